```python
import jax, jax.numpy as jnp
from jax import lax
import numpy as np

D_MODEL = 1024
BATCH = 16
SEQ = 4096
DEPTH = 2

HEAD_DIM = 64
SWA_HEADS = 8
SWA_KV_HEADS = 2
SWA_WINDOW = 128
SWA_BLOCK = SWA_WINDOW
RWKV_HEADS = 8
RWKV_DIM = RWKV_HEADS * HEAD_DIM
RWKV_W_LORA = 64
RWKV_A_LORA = 64
RWKV_G_LORA = 128
RWKV_GN_EPS = 64e-5
NSA_HEADS = 16
NSA_KV_HEADS = 4
CMP_LEN = 32
CMP_STRIDE = 16
CMP_HIDDEN = 256
SEL_LEN = 64
N_SEL = 8
NSA_WINDOW = 256
NSA_QBLOCK = 64
N_BRANCH = 3
D_FF = 2816
CONV_WIDTH = 3
NORM_EPS = 1e-6

N_EVEN = (DEPTH + 1) // 2
N_ODD = DEPTH // 2

SWA_Q_COLS = SWA_HEADS * HEAD_DIM
SWA_KV_COLS = SWA_KV_HEADS * HEAD_DIM
SWA_COLS = SWA_Q_COLS + 2 * SWA_KV_COLS
RWKV_SPLITS = (RWKV_DIM, RWKV_DIM, RWKV_DIM, RWKV_W_LORA, RWKV_A_LORA, RWKV_G_LORA)
RWKV_COLS = sum(RWKV_SPLITS)
HY_COLS = SWA_COLS + RWKV_COLS
HY_OUT = SWA_Q_COLS + RWKV_DIM
NSA_KV_COLS = NSA_KV_HEADS * HEAD_DIM
NSA_SPLITS = (NSA_HEADS * HEAD_DIM,) + (NSA_KV_COLS,) * 6 + (NSA_HEADS * N_BRANCH,)
NSA_COLS = sum(NSA_SPLITS)
NSA_OUT = NSA_HEADS * HEAD_DIM

kernel_name = "hybrid_swa_rwkv7_nsa_convffn"


def split_cols(z, sizes):
    return jnp.split(z, [int(c) for c in np.cumsum(sizes)[:-1]], axis=-1)


def rms_norm(x, g):
    xf = x.astype(jnp.float32)
    y = xf * lax.rsqrt(jnp.mean(xf * xf, axis=-1, keepdims=True) + NORM_EPS)
    return (y * g.astype(jnp.float32)).astype(x.dtype)


def alibi_slopes(n_heads):
    return jnp.exp2(-8.0 * jnp.arange(1, n_heads + 1, dtype=jnp.float32) / n_heads)


def masked_softmax(s, valid):
    s = jnp.where(valid, s, -jnp.inf)
    m = jnp.max(s, axis=-1, keepdims=True)
    m = jnp.where(jnp.isfinite(m), m, 0.0)
    p = jnp.where(valid, jnp.exp(s - m), 0.0)
    d = jnp.sum(p, axis=-1, keepdims=True)
    return p / jnp.where(d > 0, d, 1.0)


def swa_sink_attention(q, k, v, sinks, slopes):
    B, T, _ = q.shape
    G, H, D, L = SWA_KV_HEADS, SWA_HEADS, HEAD_DIM, SWA_BLOCK
    R = H // G
    nb = T // L
    f32 = jnp.float32
    qb = q.reshape(B, nb, L, G, R, D).astype(f32) * D ** -0.5

    def band(a):
        a = jnp.pad(a.reshape(B, T, G, D), ((0, 0), (L, 0), (0, 0), (0, 0))).reshape(B, nb + 1, L, G, D)
        return jnp.concatenate([a[:, :-1], a[:, 1:]], axis=2).astype(f32)

    kb, vb = band(k), band(v)
    qpos = jnp.arange(nb)[:, None, None] * L + jnp.arange(L)[None, :, None]
    kpos = jnp.arange(nb)[:, None, None] * L - L + jnp.arange(2 * L)[None, None, :]
    dist = qpos - kpos
    valid = (dist >= 0) & (dist < SWA_WINDOW) & (kpos >= 0)
    s = jnp.einsum('bnqgrd,bnkgd->bngrqk', qb, kb)
    s = s - slopes.reshape(G, R)[None, None, :, :, None, None] * dist.astype(f32)[None, :, None, None]
    s = jnp.where(valid[None, :, None, None], s, -jnp.inf)
    sink = sinks.astype(f32).reshape(G, R)[None, None, :, :, None, None]
    m = jnp.maximum(jnp.max(s, axis=-1, keepdims=True), sink)
    p = jnp.exp(s - m)
    p = p / (jnp.sum(p, axis=-1, keepdims=True) + jnp.exp(sink - m))
    o = jnp.einsum('bngrqk,bnkgd->bnqgrd', p, vb)
    return o.reshape(B, T, H * D)


def rwkv7_time_mix(z, mu, w0, w2, a0, a2, g2, k_k, k_a, r_k, ln_g, ln_b):
    B, T, _ = z.shape
    H, N = RWKV_HEADS, HEAD_DIM
    f32 = jnp.float32
    z_prev = jnp.pad(z, ((0, 0), (1, 0), (0, 0)))[:, :-1]
    z = z + (z_prev - z) * mu
    r, k, v, w_lo, a_lo, g_lo = split_cols(z, RWKV_SPLITS)
    log_w = -jax.nn.softplus(-(w0 + jnp.tanh(w_lo) @ w2).astype(f32)) - 0.5
    decay = jnp.exp(-jnp.exp(log_w))
    a = jax.nn.sigmoid((a0 + a_lo @ a2).astype(f32))
    g = (jax.nn.sigmoid(g_lo) @ g2).astype(f32)
    heads = lambda t: t.astype(f32).reshape(B, T, H, N)
    r, k, v, decay, a = heads(r), heads(k), heads(v), heads(decay), heads(a)
    kk = k * k_k.astype(f32).reshape(H, N)
    kk = kk / jnp.maximum(jnp.linalg.norm(kk, axis=-1, keepdims=True), 1e-12)
    k = k * (1.0 + (a - 1.0) * k_a.astype(f32).reshape(H, N))

    def step(S, inp):
        r_t, w_t, k_t, v_t, kk_t, a_t = inp
        sa = jnp.einsum('bhij,bhj->bhi', S, kk_t)
        S = S * w_t[:, :, None, :] - sa[..., None] * (kk_t * a_t)[:, :, None, :] + v_t[..., None] * k_t[:, :, None, :]
        return S, jnp.einsum('bhij,bhj->bhi', S, r_t)

    seq = tuple(jnp.moveaxis(t, 1, 0) for t in (r, decay, k, v, kk, a))
    _, y = lax.scan(step, jnp.zeros((B, H, N, N), f32), seq)
    y = jnp.moveaxis(y, 0, 1)
    mean = jnp.mean(y, axis=-1, keepdims=True)
    var = jnp.mean(jnp.square(y - mean), axis=-1, keepdims=True)
    y = ((y - mean) * lax.rsqrt(var + RWKV_GN_EPS)).reshape(B, T, RWKV_DIM)
    y = y * ln_g.astype(f32) + ln_b.astype(f32)
    bonus = jnp.sum(r * k * r_k.astype(f32), axis=-1, keepdims=True) * v
    return (y + bonus.reshape(B, T, RWKV_DIM)) * g


def hybrid_swa_rwkv(h, w_in, w_out, sinks, mu, w0, w2, a0, a2, g2, k_k, k_a, r_k, ln_g, ln_b):
    z = h @ w_in
    q, k, v = split_cols(z[..., :SWA_COLS], (SWA_Q_COLS, SWA_KV_COLS, SWA_KV_COLS))
    o_a = swa_sink_attention(q, k, v, sinks, alibi_slopes(SWA_HEADS))
    o_b = rwkv7_time_mix(z[..., SWA_COLS:], mu, w0, w2, a0, a2, g2, k_k, k_a, r_k, ln_g, ln_b)
    o = jnp.concatenate([o_a, o_b], axis=-1).astype(h.dtype)
    return o @ w_out


def compress_blocks(a, idx, pos, w1, w2):
    blocks = a[:, idx] + pos[None, None, :, None, :]
    hdn = jax.nn.gelu(jnp.einsum('bclgd,ldf->bcgf', blocks, w1), approximate=True)
    return jnp.einsum('bcgf,fd->bcgd', hdn, w2)


def nsa_layer(h, w_in, w_out, pos_k, w1_k, w2_k, pos_v, w1_v, w2_v):
    B, T, _ = h.shape
    G, H, D = NSA_KV_HEADS, NSA_HEADS, HEAD_DIM
    R = H // G
    Q = NSA_QBLOCK
    f32 = jnp.float32
    z = h @ w_in
    q, kc, vc, ks, vs, kw, vw, gl = split_cols(z, NSA_SPLITS)
    q = q.reshape(B, T, G, R, D).astype(f32) * D ** -0.5
    kv = lambda t: t.reshape(B, T, G, D)
    kc, vc, ks, vs, kw, vw = kv(kc), kv(vc), kv(ks), kv(vs), kv(kw), kv(vw)
    gates = jax.nn.sigmoid(gl.astype(f32)).reshape(B, T, G, R, N_BRANCH)
    slopes = alibi_slopes(H).reshape(G, R)

    n_cmp = (T - CMP_LEN) // CMP_STRIDE + 1
    cmp_idx = np.arange(n_cmp)[:, None] * CMP_STRIDE + np.arange(CMP_LEN)[None, :]
    k_cmp = compress_blocks(kc, cmp_idx, pos_k, w1_k, w2_k).astype(f32)
    v_cmp = compress_blocks(vc, cmp_idx, pos_v, w1_v, w2_v).astype(f32)
    cmp_end = jnp.asarray(cmp_idx[:, -1], jnp.int32)

    n_blk = T // SEL_LEN
    k_sel = min(N_SEL, n_blk)
    ks_blk = ks.reshape(B, n_blk, SEL_LEN, G, D).transpose(0, 3, 1, 2, 4).astype(f32)
    vs_blk = vs.reshape(B, n_blk, SEL_LEN, G, D).transpose(0, 3, 1, 2, 4).astype(f32)
    ratio, span = SEL_LEN // CMP_STRIDE, CMP_LEN // CMP_STRIDE
    ov = (ratio * np.arange(n_blk)[:, None, None] + np.arange(ratio)[None, :, None]
          - np.arange(span)[None, None, :]).reshape(n_blk, ratio * span)
    ov_valid = jnp.asarray((ov >= 0) & (ov < n_cmp), f32)
    ov = jnp.asarray(np.clip(ov, 0, n_cmp - 1), jnp.int32)
    blk_ids = jnp.arange(n_blk)
    b_ix = jnp.arange(B)[:, None, None, None]
    g_ix = jnp.arange(G)[None, :, None, None]

    kw_pad = jnp.pad(kw, ((0, 0), (NSA_WINDOW, 0), (0, 0), (0, 0))).astype(f32)
    vw_pad = jnp.pad(vw, ((0, 0), (NSA_WINDOW, 0), (0, 0), (0, 0))).astype(f32)

    def block(i):
        t0 = i * Q
        t = t0 + jnp.arange(Q)
        qi = lax.dynamic_slice_in_dim(q, t0, Q, axis=1)
        g_blk = lax.dynamic_slice_in_dim(gates, t0, Q, axis=1)
        dist_c = t[:, None] - cmp_end[None, :]
        s = jnp.einsum('bqgrd,bcgd->bgrqc', qi, k_cmp) - slopes[:, :, None, None] * dist_c.astype(f32)
        p_cmp = masked_softmax(s, dist_c >= 0)
        o_cmp = jnp.einsum('bgrqc,bcgd->bqgrd', p_cmp, v_cmp)
        imp = jnp.sum(jnp.sum(p_cmp, axis=2)[..., ov] * ov_valid, axis=-1)
        cur = (t // SEL_LEN)[:, None]
        forced = (blk_ids == 0) | (blk_ids == cur) | (blk_ids == cur - 1)
        imp = jnp.where(forced, jnp.inf, jnp.where(blk_ids > cur, -jnp.inf, imp))
        _, sel = lax.top_k(imp, k_sel)
        k_g = ks_blk[b_ix, g_ix, sel]
        v_g = vs_blk[b_ix, g_ix, sel]
        kpos = sel[..., None] * SEL_LEN + jnp.arange(SEL_LEN)
        dist_s = (t[:, None, None] - kpos)[:, :, None]
        s = jnp.einsum('bqgrd,bgqnkd->bgrqnk', qi, k_g) - slopes[None, :, :, None, None, None] * dist_s.astype(f32)
        p = masked_softmax(s.reshape(B, G, R, Q, k_sel * SEL_LEN), (dist_s >= 0).reshape(B, G, 1, Q, k_sel * SEL_LEN))
        o_slc = jnp.einsum('bgrqnk,bgqnkd->bqgrd', p.reshape(s.shape), v_g)
        kwi = lax.dynamic_slice_in_dim(kw_pad, t0, NSA_WINDOW + Q, axis=1)
        vwi = lax.dynamic_slice_in_dim(vw_pad, t0, NSA_WINDOW + Q, axis=1)
        kp = t0 - NSA_WINDOW + jnp.arange(NSA_WINDOW + Q)
        dist_w = t[:, None] - kp[None, :]
        valid_w = (dist_w >= 0) & (dist_w < NSA_WINDOW) & (kp[None, :] >= 0)
        s = jnp.einsum('bqgrd,bkgd->bgrqk', qi, kwi) - slopes[:, :, None, None] * dist_w.astype(f32)
        o_win = jnp.einsum('bgrqk,bkgd->bqgrd', masked_softmax(s, valid_w), vwi)
        return g_blk[..., 0, None] * o_cmp + g_blk[..., 1, None] * o_slc + g_blk[..., 2, None] * o_win

    o = lax.map(block, jnp.arange(T // Q))
    o = jnp.moveaxis(o, 0, 1).reshape(B, T, NSA_OUT).astype(h.dtype)
    return o @ w_out


def conv_ffn(h, w_up, conv_w, conv_b, w_down):
    T = h.shape[1]
    gate, val = jnp.split(h @ w_up, 2, axis=-1)
    gp = jnp.pad(gate, ((0, 0), (CONV_WIDTH - 1, 0), (0, 0)))
    conv = conv_b + gp[:, CONV_WIDTH - 1:] * conv_w[CONV_WIDTH - 1]
    for j in range(CONV_WIDTH - 1):
        conv = conv + gp[:, j:j + T] * conv_w[j]
    return (jax.nn.gelu(conv, approximate=True) * val) @ w_down


def setup_inputs(seed: int = 0) -> dict:
    key = jax.random.key(seed)
    keys = iter(jax.random.split(key, 48))
    f32 = jnp.float32
    nrm = lambda shape, scale: jax.random.normal(next(keys), shape, f32) * scale
    uni = lambda shape, lo, hi: jax.random.uniform(next(keys), shape, f32, lo, hi)
    NE, NO = N_EVEN, N_ODD
    return {
        "x": nrm((BATCH, SEQ, D_MODEL), 1.0),
        "mix_pre_g": 1.0 + nrm((DEPTH, D_MODEL), 0.05),
        "mix_post_g": 1.0 + nrm((DEPTH, D_MODEL), 0.05),
        "ffn_pre_g": 1.0 + nrm((DEPTH, D_MODEL), 0.05),
        "ffn_post_g": 1.0 + nrm((DEPTH, D_MODEL), 0.05),
        "hy_w_in": nrm((NE, D_MODEL, HY_COLS), D_MODEL ** -0.5),
        "hy_w_out": nrm((NE, HY_OUT, D_MODEL), HY_OUT ** -0.5),
        "swa_sinks": nrm((NE, SWA_HEADS), 0.5),
        "rwkv_mu": uni((NE, RWKV_COLS), 0.0, 1.0),
        "rwkv_w0": uni((NE, RWKV_DIM), -5.0, -1.0),
        "rwkv_w2": nrm((NE, RWKV_W_LORA, RWKV_DIM), 0.1),
        "rwkv_a0": nrm((NE, RWKV_DIM), 0.1),
        "rwkv_a2": nrm((NE, RWKV_A_LORA, RWKV_DIM), 0.1),
        "rwkv_g2": nrm((NE, RWKV_G_LORA, RWKV_DIM), RWKV_G_LORA ** -0.5),
        "rwkv_k_k": 0.85 + nrm((NE, RWKV_DIM), 0.05),
        "rwkv_k_a": 1.0 + nrm((NE, RWKV_DIM), 0.05),
        "rwkv_r_k": nrm((NE, RWKV_HEADS, HEAD_DIM), 0.1),
        "rwkv_ln_g": 1.0 + nrm((NE, RWKV_DIM), 0.05),
        "rwkv_ln_b": nrm((NE, RWKV_DIM), 0.01),
        "nsa_w_in": nrm((NO, D_MODEL, NSA_COLS), D_MODEL ** -0.5),
        "nsa_w_out": nrm((NO, NSA_OUT, D_MODEL), NSA_OUT ** -0.5),
        "nsa_cmp_pos_k": nrm((NO, CMP_LEN, HEAD_DIM), 0.1),
        "nsa_cmp_w1_k": nrm((NO, CMP_LEN, HEAD_DIM, CMP_HIDDEN), (CMP_LEN * HEAD_DIM) ** -0.5),
        "nsa_cmp_w2_k": nrm((NO, CMP_HIDDEN, HEAD_DIM), CMP_HIDDEN ** -0.5),
        "nsa_cmp_pos_v": nrm((NO, CMP_LEN, HEAD_DIM), 0.1),
        "nsa_cmp_w1_v": nrm((NO, CMP_LEN, HEAD_DIM, CMP_HIDDEN), (CMP_LEN * HEAD_DIM) ** -0.5),
        "nsa_cmp_w2_v": nrm((NO, CMP_HIDDEN, HEAD_DIM), CMP_HIDDEN ** -0.5),
        "ffn_w_up": nrm((DEPTH, D_MODEL, 2 * D_FF), D_MODEL ** -0.5),
        "ffn_conv_w": nrm((DEPTH, CONV_WIDTH, D_FF), CONV_WIDTH ** -0.5),
        "ffn_conv_b": nrm((DEPTH, D_FF), 0.01),
        "ffn_w_down": nrm((DEPTH, D_FF, D_MODEL), D_FF ** -0.5),
    }


def reference(x, mix_pre_g, mix_post_g, ffn_pre_g, ffn_post_g, hy_w_in, hy_w_out, swa_sinks,
              rwkv_mu, rwkv_w0, rwkv_w2, rwkv_a0, rwkv_a2, rwkv_g2, rwkv_k_k, rwkv_k_a, rwkv_r_k,
              rwkv_ln_g, rwkv_ln_b, nsa_w_in, nsa_w_out, nsa_cmp_pos_k, nsa_cmp_w1_k, nsa_cmp_w2_k,
              nsa_cmp_pos_v, nsa_cmp_w1_v, nsa_cmp_w2_v, ffn_w_up, ffn_conv_w, ffn_conv_b, ffn_w_down):
    for layer in range(DEPTH):
        i = layer // 2
        h = rms_norm(x, mix_pre_g[layer])
        if layer % 2 == 0:
            m = hybrid_swa_rwkv(h, hy_w_in[i], hy_w_out[i], swa_sinks[i], rwkv_mu[i], rwkv_w0[i],
                                rwkv_w2[i], rwkv_a0[i], rwkv_a2[i], rwkv_g2[i], rwkv_k_k[i],
                                rwkv_k_a[i], rwkv_r_k[i], rwkv_ln_g[i], rwkv_ln_b[i])
        else:
            m = nsa_layer(h, nsa_w_in[i], nsa_w_out[i], nsa_cmp_pos_k[i], nsa_cmp_w1_k[i],
                          nsa_cmp_w2_k[i], nsa_cmp_pos_v[i], nsa_cmp_w1_v[i], nsa_cmp_w2_v[i])
        x = x + rms_norm(m, mix_post_g[layer])
        h = rms_norm(x, ffn_pre_g[layer])
        x = x + rms_norm(conv_ffn(h, ffn_w_up[layer], ffn_conv_w[layer], ffn_conv_b[layer], ffn_w_down[layer]),
                         ffn_post_g[layer])
    return x
```

```python
import functools
import math

import numpy as np
import jax
import jax.numpy as jnp
from jax import lax
from jax.experimental import pallas as pl
from jax.experimental.pallas import tpu as pltpu

F32 = jnp.float32
BF16 = jnp.bfloat16

D_MODEL = 1024
HEAD_DIM = 64
SWA_HEADS = 8
SWA_KV_HEADS = 2
SWA_WINDOW = 128
RWKV_HEADS = 8
RWKV_DIM = RWKV_HEADS * HEAD_DIM
RWKV_W_LORA = 64
RWKV_A_LORA = 64
RWKV_G_LORA = 128
RWKV_GN_EPS = 64e-5
NSA_HEADS = 16
NSA_KV_HEADS = 4
CMP_LEN = 32
CMP_STRIDE = 16
CMP_HIDDEN = 256
SEL_LEN = 64
N_SEL = 8
NSA_WINDOW = 256
N_BRANCH = 3
D_FF = 2816
CONV_WIDTH = 3
NORM_EPS = 1e-6

SWA_Q_COLS = SWA_HEADS * HEAD_DIM
SWA_KV_COLS = SWA_KV_HEADS * HEAD_DIM
SWA_COLS = SWA_Q_COLS + 2 * SWA_KV_COLS
NSA_KV_COLS = NSA_KV_HEADS * HEAD_DIM
NSA_Q_COLS = NSA_HEADS * HEAD_DIM

LANES = 128
VMEM_LIMIT = 56 * 1024 * 1024

TQ = 128
RWKV_CHUNK = 64
RWKV_QUAD = 4
NEG_INF = float("-inf")


def _params(*sem):
    return pltpu.CompilerParams(dimension_semantics=sem, vmem_limit_bytes=VMEM_LIMIT)


def _rms(x, g):
    return x * lax.rsqrt(jnp.mean(x * x, axis=-1, keepdims=True) + NORM_EPS) * g


def _mm(a, b):
    return jnp.dot(a.astype(BF16), b.astype(BF16), preferred_element_type=F32)


def _mm_nt(a, b):
    return lax.dot_general(a.astype(BF16), b.astype(BF16), (((1,), (1,)), ((), ())),
                           preferred_element_type=F32)


def _mm_tn(a, b):
    return lax.dot_general(a.astype(BF16), b.astype(BF16), (((0,), (0,)), ((), ())),
                           preferred_element_type=F32)


def _split_bf16(x, n):
    parts = []
    for _ in range(n):
        p = x.astype(BF16)
        parts.append(p)
        x = x - p.astype(F32)
    return parts


def _mm_exact_lhs(a_bf16, x, n):
    out = None
    for p in _split_bf16(x, n):
        t = jnp.dot(a_bf16, p, preferred_element_type=F32)
        out = t if out is None else out + t
    return out


def _mm_exact_rhs(x, b_bf16, n):
    out = None
    for p in _split_bf16(x, n):
        t = jnp.dot(p, b_bf16, preferred_element_type=F32)
        out = t if out is None else out + t
    return out


def _sigmoid(x):
    return 1.0 / (1.0 + jnp.exp(-x))


def _gelu_tanh(x):
    c = math.sqrt(2.0 / math.pi)
    return 0.5 * x * (1.0 + jnp.tanh(c * (x + 0.044715 * (x * x * x))))


def _norm_proj_kernel(x_ref, g_ref, w_ref, o_ref):
    h = _rms(x_ref[...], g_ref[...])
    o_ref[...] = jnp.dot(h.astype(BF16), w_ref[...], preferred_element_type=F32)


def _norm_proj(x2d, g, w, tm=512):
    n, d = x2d.shape
    c = w.shape[1]
    return pl.pallas_call(
        _norm_proj_kernel,
        grid=(n // tm,),
        in_specs=[pl.BlockSpec((tm, d), lambda i: (i, 0)),
                  pl.BlockSpec((1, d), lambda i: (0, 0)),
                  pl.BlockSpec((d, c), lambda i: (0, 0))],
        out_specs=pl.BlockSpec((tm, c), lambda i: (i, 0)),
        out_shape=jax.ShapeDtypeStruct((n, c), F32),
        compiler_params=_params("parallel"),
        name="norm_proj",
    )(x2d, g.reshape(1, d), w)


def _out_proj_kernel(*refs, n_parts):
    o_refs = refs[:n_parts]
    w_refs = refs[n_parts:2 * n_parts]
    x_ref, g_ref, y_ref = refs[2 * n_parts:]
    m = None
    for o_ref, w_ref in zip(o_refs, w_refs):
        t = jnp.dot(o_ref[...].astype(BF16), w_ref[...], preferred_element_type=F32)
        m = t if m is None else m + t
    y_ref[...] = x_ref[...] + _rms(m, g_ref[...])


def _out_proj(parts, weights, x2d, g, tm=512):
    n, d = x2d.shape
    n_parts = len(parts)
    in_specs = ([pl.BlockSpec((tm, p.shape[1]), lambda i: (i, 0)) for p in parts]
                + [pl.BlockSpec(w.shape, lambda i: (0, 0)) for w in weights]
                + [pl.BlockSpec((tm, d), lambda i: (i, 0)), pl.BlockSpec((1, d), lambda i: (0, 0))])
    return pl.pallas_call(
        functools.partial(_out_proj_kernel, n_parts=n_parts),
        grid=(n // tm,),
        in_specs=in_specs,
        out_specs=pl.BlockSpec((tm, d), lambda i: (i, 0)),
        out_shape=jax.ShapeDtypeStruct((n, d), F32),
        compiler_params=_params("parallel"),
        name="out_proj",
    )(*parts, *weights, x2d, g.reshape(1, d))


FFN_HALO = 16


def _ffn_kernel(xh_ref, x_ref, gpre_ref, wg_ref, wv_ref, cw_ref, cb_ref, wd_ref, gpost_ref,
                y_ref, h_scr, acc_scr, *, tm, tiles_per_seq):
    i = pl.program_id(0)
    j = pl.program_id(1)
    nj = pl.num_programs(1)

    @pl.when(j == 0)
    def _():
        xe = jnp.concatenate([xh_ref[...], x_ref[...]], axis=0)
        h_scr[...] = _rms(xe, gpre_ref[...]).astype(BF16)

    h = h_scr[...]
    gate = jnp.dot(h, wg_ref[...], preferred_element_type=F32)
    val = jnp.dot(h[FFN_HALO:], wv_ref[...], preferred_element_type=F32)
    row = lax.broadcasted_iota(jnp.int32, gate.shape, 0)
    seq_start = (i % tiles_per_seq) == 0
    gate = jnp.where(jnp.logical_and(row < FFN_HALO, seq_start), 0.0, gate)
    cw = cw_ref[...]
    conv = cb_ref[...] + gate[FFN_HALO:] * cw[2:3]
    conv = conv + gate[FFN_HALO - 2:-2] * cw[0:1]
    conv = conv + gate[FFN_HALO - 1:-1] * cw[1:2]
    act = _gelu_tanh(conv) * val
    part = jnp.dot(act.astype(BF16), wd_ref[...], preferred_element_type=F32)

    @pl.when(j == 0)
    def _():
        acc_scr[...] = part

    @pl.when(j > 0)
    def _():
        acc_scr[...] += part

    @pl.when(j == nj - 1)
    def _():
        y_ref[...] = x_ref[...] + _rms(acc_scr[...], gpost_ref[...])


def _conv_ffn(x2d, seq_len, g_pre, w_up, conv_w, conv_b, w_down, g_post, tm=512, nf=2):
    n, d = x2d.shape
    tf = D_FF // nf
    hb = tm // FFN_HALO
    return pl.pallas_call(
        functools.partial(_ffn_kernel, tm=tm, tiles_per_seq=seq_len // tm),
        grid=(n // tm, nf),
        in_specs=[pl.BlockSpec((FFN_HALO, d), lambda i, j: (jnp.maximum(i * hb - 1, 0), 0)),
                  pl.BlockSpec((tm, d), lambda i, j: (i, 0)),
                  pl.BlockSpec((1, d), lambda i, j: (0, 0)),
                  pl.BlockSpec((d, tf), lambda i, j: (0, j)),
                  pl.BlockSpec((d, tf), lambda i, j: (0, nf + j)),
                  pl.BlockSpec((CONV_WIDTH, tf), lambda i, j: (0, j)),
                  pl.BlockSpec((1, tf), lambda i, j: (0, j)),
                  pl.BlockSpec((tf, d), lambda i, j: (j, 0)),
                  pl.BlockSpec((1, d), lambda i, j: (0, 0))],
        out_specs=pl.BlockSpec((tm, d), lambda i, j: (i, 0)),
        out_shape=jax.ShapeDtypeStruct((n, d), F32),
        scratch_shapes=[pltpu.VMEM((tm + FFN_HALO, d), BF16), pltpu.VMEM((tm, d), F32)],
        compiler_params=_params("parallel", "arbitrary"),
        name="conv_ffn",
    )(x2d, x2d, g_pre.reshape(1, d), w_up, w_up, conv_w, conv_b.reshape(1, D_FF), w_down,
      g_post.reshape(1, d))


def _swa_kernel(q_ref, k_ref, v_ref, slope_ref, sink_ref, o_ref, *, n_rep, window):
    i = pl.program_id(2)
    t0 = i * TQ
    nk = window + TQ
    start = pl.multiple_of(jnp.maximum(t0 - window, 0), LANES)
    q_t = jnp.concatenate([q_ref[0, 0, r] for r in range(n_rep)], axis=1)
    kb = k_ref[0, 0, pl.ds(start, nk), :]
    vb = v_ref[0, 0, :, pl.ds(start, nk)]
    s = jnp.dot(kb, q_t, preferred_element_type=F32)
    kpos = start + lax.broadcasted_iota(jnp.int32, s.shape, 0)
    tpos = t0 + (lax.broadcasted_iota(jnp.int32, s.shape, 1) & (TQ - 1))
    dist = tpos - kpos
    valid = jnp.logical_and(dist >= 0, dist < window)
    s = jnp.where(valid, s - slope_ref[0] * dist.astype(F32), NEG_INF)
    sink = sink_ref[0]
    m = jnp.maximum(jnp.max(s, axis=0, keepdims=True), sink)
    p = jnp.exp(s - m)
    den = jnp.sum(p, axis=0, keepdims=True) + jnp.exp(sink - m)
    o_t = jnp.dot(vb, p.astype(BF16), preferred_element_type=F32) / den
    o_ref[0] = jnp.concatenate([o_t[:, r * TQ:(r + 1) * TQ].T for r in range(n_rep)], axis=1)


def _swa_attention(q_t, k, v_t, slope_row, sink_row):
    b, g, r, d, t = q_t.shape
    return pl.pallas_call(
        functools.partial(_swa_kernel, n_rep=r, window=SWA_WINDOW),
        grid=(b, g, t // TQ),
        in_specs=[pl.BlockSpec((1, 1, r, d, TQ), lambda bi, gi, i: (bi, gi, 0, 0, i)),
                  pl.BlockSpec((1, 1, t, d), lambda bi, gi, i: (bi, gi, 0, 0)),
                  pl.BlockSpec((1, 1, d, t), lambda bi, gi, i: (bi, gi, 0, 0)),
                  pl.BlockSpec((1, 1, r * TQ), lambda bi, gi, i: (gi, 0, 0)),
                  pl.BlockSpec((1, 1, r * TQ), lambda bi, gi, i: (gi, 0, 0))],
        out_specs=pl.BlockSpec((1, TQ, r * d), lambda bi, gi, i: (bi, i, gi)),
        out_shape=jax.ShapeDtypeStruct((b, t, g * r * d), F32),
        compiler_params=_params("parallel", "parallel", "parallel"),
        name="swa_attention",
    )(q_t, k, v_t, slope_row, sink_row)


def _rwkv_kernel(zr_ref, zk_ref, zv_ref, zl_ref, mur_ref, muk_ref, muv_ref, mul_ref,
                 w0_ref, a0_ref, kk_ref, ka_ref, rk_ref, lng_ref, lnb_ref,
                 w2_ref, a2_ref, g2_ref, o_ref, prev_scr, s_scr):
    c = pl.program_id(2)
    cs = RWKV_CHUNK
    wd = RWKV_QUAD * HEAD_DIM

    @pl.when(c == 0)
    def _():
        prev_scr[...] = jnp.zeros_like(prev_scr)
        s_scr[...] = jnp.zeros_like(s_scr)

    row = lax.broadcasted_iota(jnp.int32, (cs, wd), 0)

    def shifted(z_ref, mu_ref, idx):
        z = z_ref[0]
        prev = prev_scr[idx, 0:1, :]
        zp = jnp.where(row == 0, prev, pltpu.roll(z, 1, 0))
        prev_scr[idx, 0:1, :] = z[cs - 1:cs, :]
        return z + (zp - z) * mu_ref[...]

    r = shifted(zr_ref, mur_ref, 0)
    k = shifted(zk_ref, muk_ref, 1)
    v = shifted(zv_ref, muv_ref, 2)
    lo = shifted(zl_ref, mul_ref, 3)

    yw = w0_ref[...] + _mm(jnp.tanh(lo), w2_ref[...])
    ya = a0_ref[...] + _mm(lo, a2_ref[...])
    gate = _mm(_sigmoid(lo), g2_ref[...])
    softplus = jnp.maximum(-yw, 0.0) + jnp.log(1.0 + jnp.exp(-jnp.abs(yw)))
    logd = -jnp.exp(-softplus - 0.5)
    a = _sigmoid(ya)

    i0 = lax.broadcasted_iota(jnp.int32, (wd, wd), 0)
    i1 = lax.broadcasted_iota(jnp.int32, (wd, wd), 1)
    same_head = (i0 // HEAD_DIM) == (i1 // HEAD_DIM)
    ones_head = jnp.where(same_head, 1.0, 0.0).astype(BF16)

    def head_sum(x):
        return _mm_exact_rhs(x, ones_head, 2)

    kk = k * kk_ref[...]
    kk = kk / jnp.maximum(jnp.sqrt(head_sum(kk * kk)), 1e-12)
    kmod = k * (1.0 + (a - 1.0) * ka_ref[...])
    b = kk * a

    c0 = lax.broadcasted_iota(jnp.int32, (cs, cs), 0)
    c1 = lax.broadcasted_iota(jnp.int32, (cs, cs), 1)
    tri = jnp.where(c0 >= c1, 1.0, 0.0).astype(BF16)
    cl = _mm_exact_lhs(tri, logd, 3)
    cl_end = cl[cs - 1:cs, :]
    w_inc = jnp.exp(cl)
    w_exc = jnp.exp(cl - logd)
    w_inv = jnp.exp(-cl)
    w_rem = jnp.exp(cl_end - cl)

    def stack(x):
        return jnp.where(same_head, jnp.concatenate([x] * RWKV_QUAD, axis=0), 0.0)

    x_kk = stack(kk * w_exc)
    x_r = stack(r * w_inc)
    x_b = stack(b * w_inv)
    x_k = stack(kmod * w_inv)
    x_v = stack(v)
    x_bd = stack(b * w_rem)
    x_kd = stack(kmod * w_rem)

    strict = (i0 % cs) > (i1 % cs)
    incl = (i0 % cs) >= (i1 % cs)
    a_ub = jnp.where(strict, _mm_nt(x_kk, x_b), 0.0)
    a_uk = jnp.where(strict, _mm_nt(x_kk, x_k), 0.0)
    a_rb = jnp.where(incl, _mm_nt(x_r, x_b), 0.0)
    a_rk = jnp.where(incl, _mm_nt(x_r, x_k), 0.0)

    eye = i0 == i1
    pw = a_ub
    t_inv = jnp.where(eye, 1.0, 0.0) - a_ub
    for _ in range(int(math.log2(cs)) - 1):
        pw = _mm(pw, pw)
        t_inv = t_inv + _mm(t_inv, pw)

    p1 = _mm(t_inv, x_kk)
    p2 = _mm(t_inv, _mm(a_uk, x_v))
    q1 = x_r - _mm(a_rb, p1)
    q2 = _mm(a_rk, x_v) - _mm(a_rb, p2)
    g_mat = jnp.where(eye, jnp.exp(cl_end), 0.0) - _mm_tn(x_bd, p1)
    h_mat = _mm_tn(x_kd, x_v) - _mm_tn(x_bd, p2)

    s0 = s_scr[...]
    ys = _mm(q1, s0) + q2
    s_scr[...] = _mm(g_mat, s0) + h_mat
    y = ys[0:cs]
    for hd in range(1, RWKV_QUAD):
        y = y + ys[hd * cs:(hd + 1) * cs]

    inv_n = 1.0 / HEAD_DIM
    mean = head_sum(y) * inv_n
    yc = y - mean
    var = head_sum(yc * yc) * inv_n
    yn = yc * lax.rsqrt(var + RWKV_GN_EPS) * lng_ref[...] + lnb_ref[...]
    bonus = head_sum(r * kmod * rk_ref[...]) * v
    o_ref[0] = (yn + bonus) * gate


def _rwkv_time_mix(z3, mu, w0, w2, a0, a2, g2, k_k, k_a, r_k, ln_g, ln_b):
    b, t, _ = z3.shape
    wd = RWKV_QUAD * HEAD_DIM
    nq = RWKV_DIM // wd
    base = SWA_COLS // wd
    lora_rows = RWKV_W_LORA + RWKV_A_LORA + RWKV_G_LORA
    assert lora_rows == wd and SWA_COLS % wd == 0

    def pad_rows(w, start):
        return jnp.zeros((wd, RWKV_DIM), F32).at[start:start + w.shape[0]].set(w).astype(BF16)

    w2p = pad_rows(w2, 0)
    a2p = pad_rows(a2, RWKV_W_LORA)
    g2p = pad_rows(g2, RWKV_W_LORA + RWKV_A_LORA)
    mu2 = mu.reshape(1, -1)
    row = lambda x: x.reshape(1, RWKV_DIM)
    zspec = lambda off: pl.BlockSpec((1, RWKV_CHUNK, wd), lambda bi, qi, ci: (bi, ci, base + off + qi))
    zlspec = pl.BlockSpec((1, RWKV_CHUNK, wd), lambda bi, qi, ci: (bi, ci, base + 3 * nq))
    muspec = lambda off: pl.BlockSpec((1, wd), lambda bi, qi, ci: (0, off + qi))
    mulspec = pl.BlockSpec((1, wd), lambda bi, qi, ci: (0, 3 * nq))
    pspec = pl.BlockSpec((1, wd), lambda bi, qi, ci: (0, qi))
    wspec = pl.BlockSpec((wd, wd), lambda bi, qi, ci: (0, qi))
    return pl.pallas_call(
        _rwkv_kernel,
        grid=(b, nq, t // RWKV_CHUNK),
        in_specs=[zspec(0), zspec(nq), zspec(2 * nq), zlspec,
                  muspec(0), muspec(nq), muspec(2 * nq), mulspec,
                  pspec, pspec, pspec, pspec, pspec, pspec, pspec,
                  wspec, wspec, wspec],
        out_specs=pl.BlockSpec((1, RWKV_CHUNK, wd), lambda bi, qi, ci: (bi, ci, qi)),
        out_shape=jax.ShapeDtypeStruct((b, t, RWKV_DIM), F32),
        scratch_shapes=[pltpu.VMEM((4, 8, wd), F32), pltpu.VMEM((wd, wd), F32)],
        compiler_params=_params("parallel", "parallel", "arbitrary"),
        name="rwkv7_time_mix",
    )(z3, z3, z3, z3, mu2, mu2, mu2, mu2,
      row(w0), row(a0), row(k_k), row(k_a), row(r_k), row(ln_g), row(ln_b),
      w2p, a2p, g2p)


def _compress_kernel(kb_ref, vb_ref, posk_ref, posv_ref, w1k_ref, w1v_ref, w2k_ref, w2v_ref,
                     kc_ref, vc_ref, *, n_cmp):
    half = CMP_STRIDE * HEAD_DIM

    def run(blk_ref, pos_ref, w1_ref, w2_ref):
        a = blk_ref[0, 0]
        lo = _mm(a + pos_ref[0:1, :], w1_ref[0:half, :])
        hi = _mm(a + pos_ref[1:2, :], w1_ref[half:2 * half, :])
        n_half = a.shape[0]
        hid = lo + pltpu.roll(hi, n_half - 1, 0)
        out = _mm(_gelu_tanh(hid), w2_ref[...])
        rows = lax.broadcasted_iota(jnp.int32, out.shape, 0)
        return jnp.where(rows < n_cmp, out, 0.0)

    kc_ref[0, 0] = run(kb_ref, posk_ref, w1k_ref, w2k_ref).astype(BF16)
    vc_ref[0, 0] = run(vb_ref, posv_ref, w1v_ref, w2v_ref).T.astype(BF16)


def _compress(kc_blocks, vc_blocks, pos_k, pos_v, w1_k, w1_v, w2_k, w2_v, n_cmp):
    b, g, nh, hw = kc_blocks.shape
    blk = pl.BlockSpec((1, 1, nh, hw), lambda bi, gi: (bi, gi, 0, 0))
    full = lambda a: pl.BlockSpec(a.shape, lambda bi, gi: (0,) * a.ndim)
    return pl.pallas_call(
        functools.partial(_compress_kernel, n_cmp=n_cmp),
        grid=(b, g),
        in_specs=[blk, blk, full(pos_k), full(pos_v), full(w1_k), full(w1_v), full(w2_k), full(w2_v)],
        out_specs=[pl.BlockSpec((1, 1, nh, HEAD_DIM), lambda bi, gi: (bi, gi, 0, 0)),
                   pl.BlockSpec((1, 1, HEAD_DIM, nh), lambda bi, gi: (bi, gi, 0, 0))],
        out_shape=[jax.ShapeDtypeStruct((b, g, nh, HEAD_DIM), BF16),
                   jax.ShapeDtypeStruct((b, g, HEAD_DIM, nh), BF16)],
        compiler_params=_params("parallel", "parallel"),
        name="nsa_compress",
    )(kc_blocks, vc_blocks, pos_k, pos_v, w1_k, w1_v, w2_k, w2_v)


def _masked_softmax_cols(s, valid):
    s = jnp.where(valid, s, NEG_INF)
    m = jnp.max(s, axis=0, keepdims=True)
    m = jnp.where(m == NEG_INF, 0.0, m)
    p = jnp.where(valid, jnp.exp(s - m), 0.0)
    d = jnp.sum(p, axis=0, keepdims=True)
    return p / jnp.where(d > 0, d, 1.0)


def _nsa_kernel(q_ref, kc_ref, vc_ref, ks_ref, vs_ref, kw_ref, vw_ref, gl_ref, slope_ref, ovw_ref,
                o_ref, sel_scr, *, n_rep, n_blk):
    i = pl.program_id(2)
    t0 = i * TQ
    wq = n_rep * TQ
    q_t = jnp.concatenate([q_ref[0, 0, r] for r in range(n_rep)], axis=1)
    slope = slope_ref[0]
    lane_t = lax.broadcasted_iota(jnp.int32, (1, wq), 1) & (TQ - 1)
    tpos = t0 + lane_t

    kc = kc_ref[0, 0]
    s = jnp.dot(kc, q_t, preferred_element_type=F32)
    cmp_end = lax.broadcasted_iota(jnp.int32, s.shape, 0) * CMP_STRIDE + (CMP_LEN - 1)
    dist = tpos - cmp_end
    p_cmp = _masked_softmax_cols(s - slope * dist.astype(F32), dist >= 0)
    o_cmp = jnp.dot(vc_ref[0, 0], p_cmp.astype(BF16), preferred_element_type=F32)

    p_sum = p_cmp[:, 0:TQ]
    for r in range(1, n_rep):
        p_sum = p_sum + p_cmp[:, r * TQ:(r + 1) * TQ]
    imp = _mm_exact_lhs(ovw_ref[...], p_sum, 3)
    blk = lax.broadcasted_iota(jnp.int32, (n_blk, TQ), 0)
    cur = (t0 + lax.broadcasted_iota(jnp.int32, (n_blk, TQ), 1)) // SEL_LEN
    forced = jnp.logical_or(blk == 0, jnp.logical_or(blk == cur, blk == cur - 1))
    score = jnp.where(forced, jnp.inf, jnp.where(blk > cur, NEG_INF, imp))
    sel = jnp.zeros((n_blk, TQ), F32)
    for _ in range(min(N_SEL, n_blk)):
        best = jnp.max(score, axis=0, keepdims=True)
        first = jnp.min(jnp.where(score == best, blk, n_blk), axis=0, keepdims=True)
        hit = blk == first
        sel = jnp.where(hit, 1.0, sel)
        score = jnp.where(hit, NEG_INF, score)
    sel_scr[...] = sel

    nk = NSA_WINDOW + TQ
    start = pl.multiple_of(jnp.maximum(t0 - NSA_WINDOW, 0), LANES)
    kb = kw_ref[0, 0, pl.ds(start, nk), :]
    vb = vw_ref[0, 0, :, pl.ds(start, nk)]
    s = jnp.dot(kb, q_t, preferred_element_type=F32)
    dist = tpos - (start + lax.broadcasted_iota(jnp.int32, s.shape, 0))
    valid = jnp.logical_and(dist >= 0, dist < NSA_WINDOW)
    p_win = _masked_softmax_cols(s - slope * dist.astype(F32), valid)
    o_win = jnp.dot(vb, p_win.astype(BF16), preferred_element_type=F32)

    ck = 2 * SEL_LEN
    key_in_chunk = lax.broadcasted_iota(jnp.int32, (ck, TQ), 0)

    def body(jp, carry):
        m_old, l_old, acc = carry
        k0 = pl.multiple_of(jp * ck, LANES)
        kb = ks_ref[0, 0, pl.ds(k0, ck), :]
        vb = vs_ref[0, 0, :, pl.ds(k0, ck)]
        s = jnp.dot(kb, q_t, preferred_element_type=F32)
        rows = sel_scr[pl.ds(pl.multiple_of(2 * jp, 2), 2), :]
        picked = jnp.where(key_in_chunk < SEL_LEN, rows[0:1, :], rows[1:2, :])
        picked = jnp.concatenate([picked] * n_rep, axis=1)
        dist = tpos - (k0 + lax.broadcasted_iota(jnp.int32, s.shape, 0))
        valid = jnp.logical_and(picked > 0.5, dist >= 0)
        s = jnp.where(valid, s - slope * dist.astype(F32), NEG_INF)
        m_new = jnp.maximum(m_old, jnp.max(s, axis=0, keepdims=True))
        m_safe = jnp.where(m_new == NEG_INF, 0.0, m_new)
        alpha = jnp.exp(m_old - m_safe)
        p = jnp.where(valid, jnp.exp(s - m_safe), 0.0)
        l_new = alpha * l_old + jnp.sum(p, axis=0, keepdims=True)
        acc = alpha * acc + jnp.dot(vb, p.astype(BF16), preferred_element_type=F32)
        return m_new, l_new, acc

    init = (jnp.full((1, wq), NEG_INF, F32), jnp.zeros((1, wq), F32), jnp.zeros((HEAD_DIM, wq), F32))
    _, l_fin, acc = lax.fori_loop(0, i + 1, body, init)
    o_slc = acc / jnp.where(l_fin > 0, l_fin, 1.0)

    gates = _sigmoid(gl_ref[0, 0])
    outs = []
    for r in range(n_rep):
        sl = slice(r * TQ, (r + 1) * TQ)
        g0 = gates[N_BRANCH * r + 0:N_BRANCH * r + 1, :]
        g1 = gates[N_BRANCH * r + 1:N_BRANCH * r + 2, :]
        g2 = gates[N_BRANCH * r + 2:N_BRANCH * r + 3, :]
        outs.append((g0 * o_cmp[:, sl] + g1 * o_slc[:, sl] + g2 * o_win[:, sl]).T)
    o_ref[0] = jnp.concatenate(outs, axis=1)


def _nsa_attention(q_t, kc, vc_t, ks, vs_t, kw, vw_t, gl_t, slope_row, ov_w):
    b, g, r, d, t = q_t.shape
    n_blk = t // SEL_LEN
    ncp = kc.shape[2]
    idx4 = lambda bi, gi, i: (bi, gi, 0, 0)
    return pl.pallas_call(
        functools.partial(_nsa_kernel, n_rep=r, n_blk=n_blk),
        grid=(b, g, t // TQ),
        in_specs=[pl.BlockSpec((1, 1, r, d, TQ), lambda bi, gi, i: (bi, gi, 0, 0, i)),
                  pl.BlockSpec((1, 1, ncp, d), idx4),
                  pl.BlockSpec((1, 1, d, ncp), idx4),
                  pl.BlockSpec((1, 1, t, d), idx4),
                  pl.BlockSpec((1, 1, d, t), idx4),
                  pl.BlockSpec((1, 1, t, d), idx4),
                  pl.BlockSpec((1, 1, d, t), idx4),
                  pl.BlockSpec((1, 1, r * N_BRANCH, TQ), lambda bi, gi, i: (bi, gi, 0, i)),
                  pl.BlockSpec((1, 1, r * TQ), lambda bi, gi, i: (gi, 0, 0)),
                  pl.BlockSpec(ov_w.shape, lambda bi, gi, i: (0, 0))],
        out_specs=pl.BlockSpec((1, TQ, r * d), lambda bi, gi, i: (bi, i, gi)),
        out_shape=jax.ShapeDtypeStruct((b, t, g * r * d), F32),
        scratch_shapes=[pltpu.VMEM((n_blk, TQ), F32)],
        compiler_params=_params("parallel", "parallel", "arbitrary"),
        name="nsa_attention",
    )(q_t, kc, vc_t, ks, vs_t, kw, vw_t, gl_t, slope_row, ov_w)


def _alibi_rows(n_heads, n_groups):
    slopes = jnp.exp2(-8.0 * jnp.arange(1, n_heads + 1, dtype=F32) / n_heads)
    r = n_heads // n_groups
    return jnp.repeat(slopes.reshape(n_groups, r), TQ, axis=1).reshape(n_groups, 1, r * TQ)


def _heads_feature_major(a, b, t, g):
    return a.reshape(b, t, g, HEAD_DIM).transpose(0, 2, 3, 1)


def _heads_token_major(a, b, t, g):
    return a.reshape(b, t, g, HEAD_DIM).transpose(0, 2, 1, 3)


def _hybrid_layer(x2d, b, t, g_pre, g_post, w_in, w_out, sinks, mu, w0, w2, a0, a2, g2, k_k, k_a,
                  r_k, ln_g, ln_b):
    z = _norm_proj(x2d, g_pre, w_in.astype(BF16))
    gq, rq = SWA_KV_HEADS, SWA_HEADS // SWA_KV_HEADS
    scale = HEAD_DIM ** -0.5
    q_t = (z[:, :SWA_Q_COLS] * scale).astype(BF16).reshape(b, t, gq, rq, HEAD_DIM).transpose(0, 2, 3, 4, 1)
    k = _heads_token_major(z[:, SWA_Q_COLS:SWA_Q_COLS + SWA_KV_COLS].astype(BF16), b, t, gq)
    v_t = _heads_feature_major(z[:, SWA_Q_COLS + SWA_KV_COLS:SWA_COLS].astype(BF16), b, t, gq)
    sink_row = jnp.repeat(sinks.astype(F32).reshape(gq, rq), TQ, axis=1).reshape(gq, 1, rq * TQ)
    o_a = _swa_attention(q_t, k, v_t, _alibi_rows(SWA_HEADS, gq), sink_row)
    o_b = _rwkv_time_mix(z.reshape(b, t, -1), mu, w0, w2, a0, a2, g2, k_k, k_a, r_k.reshape(-1), ln_g, ln_b)
    w_out = w_out.astype(BF16)
    return _out_proj([o_a.reshape(b * t, -1), o_b.reshape(b * t, -1)],
                     [w_out[:SWA_Q_COLS], w_out[SWA_Q_COLS:]], x2d, g_post)


def _overlap_weights(n_blk, n_cmp_pad, n_cmp):
    ratio, span = SEL_LEN // CMP_STRIDE, CMP_LEN // CMP_STRIDE
    w = np.zeros((n_blk, n_cmp_pad), np.float32)
    for j in range(n_blk):
        for a in range(ratio):
            for s in range(span):
                c = ratio * j + a - s
                if 0 <= c < n_cmp:
                    w[j, c] += 1.0
    return jnp.asarray(w, BF16)


def _nsa_layer(x2d, b, t, g_pre, g_post, w_in, w_out, pos_k, w1_k, w2_k, pos_v, w1_v, w2_v):
    g, r = NSA_KV_HEADS, NSA_HEADS // NSA_KV_HEADS
    n_cols = w_in.shape[1]
    pad = (-n_cols) % LANES
    w_in = jnp.pad(w_in.astype(BF16), ((0, 0), (0, pad)))
    z = _norm_proj(x2d, g_pre, w_in)
    off = NSA_Q_COLS
    cols = lambda j: z[:, off + j * NSA_KV_COLS:off + (j + 1) * NSA_KV_COLS]
    scale = HEAD_DIM ** -0.5
    q_t = (z[:, :NSA_Q_COLS] * scale).astype(BF16).reshape(b, t, g, r, HEAD_DIM).transpose(0, 2, 3, 4, 1)
    n_half = t // CMP_STRIDE
    n_cmp = (t - CMP_LEN) // CMP_STRIDE + 1
    half_rows = lambda a: a.reshape(b, n_half, CMP_STRIDE, g, HEAD_DIM).transpose(0, 3, 1, 2, 4).reshape(
        b, g, n_half, CMP_STRIDE * HEAD_DIM)
    flat_pos = lambda p: p.reshape(CMP_LEN // CMP_STRIDE, CMP_STRIDE * HEAD_DIM)
    flat_w1 = lambda w: w.reshape(CMP_LEN * HEAD_DIM, CMP_HIDDEN).astype(BF16)
    kc, vc_t = _compress(half_rows(cols(0)), half_rows(cols(1)), flat_pos(pos_k), flat_pos(pos_v),
                         flat_w1(w1_k), flat_w1(w1_v), w2_k.astype(BF16), w2_v.astype(BF16), n_cmp)
    ks = _heads_token_major(cols(2).astype(BF16), b, t, g)
    vs_t = _heads_feature_major(cols(3).astype(BF16), b, t, g)
    kw = _heads_token_major(cols(4).astype(BF16), b, t, g)
    vw_t = _heads_feature_major(cols(5).astype(BF16), b, t, g)
    gl0 = off + 6 * NSA_KV_COLS
    gl_t = z[:, gl0:gl0 + NSA_HEADS * N_BRANCH].reshape(b, t, g, r * N_BRANCH).transpose(0, 2, 3, 1)
    ov_w = _overlap_weights(t // SEL_LEN, n_half, n_cmp)
    o = _nsa_attention(q_t, kc, vc_t, ks, vs_t, kw, vw_t, gl_t, _alibi_rows(NSA_HEADS, g), ov_w)
    return _out_proj([o.reshape(b * t, -1)], [w_out.astype(BF16)], x2d, g_post)


def kernel(x, mix_pre_g, mix_post_g, ffn_pre_g, ffn_post_g, hy_w_in, hy_w_out, swa_sinks, rwkv_mu, rwkv_w0, rwkv_w2, rwkv_a0, rwkv_a2, rwkv_g2, rwkv_k_k, rwkv_k_a, rwkv_r_k, rwkv_ln_g, rwkv_ln_b, nsa_w_in, nsa_w_out, nsa_cmp_pos_k, nsa_cmp_w1_k, nsa_cmp_w2_k, nsa_cmp_pos_v, nsa_cmp_w1_v, nsa_cmp_w2_v, ffn_w_up, ffn_conv_w, ffn_conv_b, ffn_w_down):
    b, t, d = x.shape
    depth = mix_pre_g.shape[0]
    x2d = x.reshape(b * t, d)
    for layer in range(depth):
        i = layer // 2
        if layer % 2 == 0:
            x2d = _hybrid_layer(x2d, b, t, mix_pre_g[layer], mix_post_g[layer], hy_w_in[i], hy_w_out[i],
                                swa_sinks[i], rwkv_mu[i], rwkv_w0[i], rwkv_w2[i], rwkv_a0[i], rwkv_a2[i],
                                rwkv_g2[i], rwkv_k_k[i], rwkv_k_a[i], rwkv_r_k[i], rwkv_ln_g[i], rwkv_ln_b[i])
        else:
            x2d = _nsa_layer(x2d, b, t, mix_pre_g[layer], mix_post_g[layer], nsa_w_in[i], nsa_w_out[i],
                             nsa_cmp_pos_k[i], nsa_cmp_w1_k[i], nsa_cmp_w2_k[i],
                             nsa_cmp_pos_v[i], nsa_cmp_w1_v[i], nsa_cmp_w2_v[i])
        x2d = _conv_ffn(x2d, t, ffn_pre_g[layer], ffn_w_up[layer].astype(BF16), ffn_conv_w[layer],
                        ffn_conv_b[layer], ffn_w_down[layer].astype(BF16), ffn_post_g[layer])
    return x2d.reshape(b, t, d)
```

```python
import functools
import math

import numpy as np
import jax
import jax.numpy as jnp
from jax import lax
from jax.experimental import pallas as pl
from jax.experimental.pallas import tpu as pltpu

F32 = jnp.float32
BF16 = jnp.bfloat16

D_MODEL = 1024
HEAD_DIM = 64
SWA_HEADS = 8
SWA_KV_HEADS = 2
SWA_WINDOW = 128
RWKV_HEADS = 8
RWKV_DIM = RWKV_HEADS * HEAD_DIM
RWKV_W_LORA = 64
RWKV_A_LORA = 64
RWKV_G_LORA = 128
RWKV_GN_EPS = 64e-5
NSA_HEADS = 16
NSA_KV_HEADS = 4
CMP_LEN = 32
CMP_STRIDE = 16
CMP_HIDDEN = 256
SEL_LEN = 64
N_SEL = 8
NSA_WINDOW = 256
N_BRANCH = 3
D_FF = 2816
CONV_WIDTH = 3
NORM_EPS = 1e-6

SWA_Q_COLS = SWA_HEADS * HEAD_DIM
SWA_KV_COLS = SWA_KV_HEADS * HEAD_DIM
SWA_COLS = SWA_Q_COLS + 2 * SWA_KV_COLS
NSA_KV_COLS = NSA_KV_HEADS * HEAD_DIM
NSA_Q_COLS = NSA_HEADS * HEAD_DIM

LANES = 128
VMEM_LIMIT = 56 * 1024 * 1024

TQ = 256
ATT_CHUNK = 128
POS_PERIOD = 512
KEY_AUG = 16
VAL_AUG = 16
SEL_GROUP = POS_PERIOD
RWKV_CHUNK = 64
RWKV_QUAD = 4
NEG_INF = float("-inf")
LOG2E = math.log2(math.e)


def _params(*sem):
    return pltpu.CompilerParams(dimension_semantics=sem, vmem_limit_bytes=VMEM_LIMIT)


def _rms(x, g):
    return x * lax.rsqrt(jnp.mean(x * x, axis=-1, keepdims=True) + NORM_EPS) * g


def _mm(a, b):
    return jnp.dot(a.astype(BF16), b.astype(BF16), preferred_element_type=F32)


def _mm_nt(a, b):
    return lax.dot_general(a.astype(BF16), b.astype(BF16), (((1,), (1,)), ((), ())),
                           preferred_element_type=F32)


def _mm_tn(a, b):
    return lax.dot_general(a.astype(BF16), b.astype(BF16), (((0,), (0,)), ((), ())),
                           preferred_element_type=F32)


def _split_bf16(x, n):
    parts = []
    for _ in range(n):
        p = x.astype(BF16)
        parts.append(p)
        x = x - p.astype(F32)
    return parts


def _mm_exact_lhs(a_bf16, x, n):
    out = None
    for p in _split_bf16(x, n):
        t = jnp.dot(a_bf16, p, preferred_element_type=F32)
        out = t if out is None else out + t
    return out


def _mm_exact_rhs(x, b_bf16, n):
    out = None
    for p in _split_bf16(x, n):
        t = jnp.dot(p, b_bf16, preferred_element_type=F32)
        out = t if out is None else out + t
    return out


def _sigmoid(x):
    return 1.0 / (1.0 + jnp.exp(-x))


def _gelu_tanh(x):
    c = math.sqrt(2.0 / math.pi)
    return 0.5 * x * (1.0 + jnp.tanh(c * (x + 0.044715 * (x * x * x))))


def _norm_proj_kernel(x_ref, g_ref, w_ref, o_ref):
    h = _rms(x_ref[...], g_ref[...])
    o_ref[...] = jnp.dot(h.astype(BF16), w_ref[...], preferred_element_type=F32)


def _norm_proj(x2d, g, w, tm=512):
    n, d = x2d.shape
    c = w.shape[1]
    return pl.pallas_call(
        _norm_proj_kernel,
        grid=(n // tm,),
        in_specs=[pl.BlockSpec((tm, d), lambda i: (i, 0)),
                  pl.BlockSpec((1, d), lambda i: (0, 0)),
                  pl.BlockSpec((d, c), lambda i: (0, 0))],
        out_specs=pl.BlockSpec((tm, c), lambda i: (i, 0)),
        out_shape=jax.ShapeDtypeStruct((n, c), F32),
        compiler_params=_params("parallel"),
        name="norm_proj",
    )(x2d, g.reshape(1, d), w)


def _out_proj_kernel(*refs, feature_major):
    n_parts = len(feature_major)
    o_refs = refs[:n_parts]
    w_refs = refs[n_parts:2 * n_parts]
    x_ref, g_ref, y_ref = refs[2 * n_parts:]
    m = None
    for o_ref, w_ref, fm in zip(o_refs, w_refs, feature_major):
        if fm:
            t = lax.dot_general(o_ref[0].astype(BF16), w_ref[...], (((0,), (0,)), ((), ())),
                                preferred_element_type=F32)
        else:
            t = jnp.dot(o_ref[0].astype(BF16), w_ref[...], preferred_element_type=F32)
        m = t if m is None else m + t
    y_ref[0] = x_ref[0] + _rms(m, g_ref[...])


def _out_proj(parts, feature_major, weights, x3d, g, tm=512):
    b, t, d = x3d.shape
    in_specs = []
    for p, fm in zip(parts, feature_major):
        if fm:
            in_specs.append(pl.BlockSpec((1, p.shape[1], tm), lambda bi, j: (bi, 0, j)))
        else:
            in_specs.append(pl.BlockSpec((1, tm, p.shape[2]), lambda bi, j: (bi, j, 0)))
    in_specs += [pl.BlockSpec(w.shape, lambda bi, j: (0, 0)) for w in weights]
    in_specs += [pl.BlockSpec((1, tm, d), lambda bi, j: (bi, j, 0)), pl.BlockSpec((1, d), lambda bi, j: (0, 0))]
    return pl.pallas_call(
        functools.partial(_out_proj_kernel, feature_major=tuple(feature_major)),
        grid=(b, t // tm),
        in_specs=in_specs,
        out_specs=pl.BlockSpec((1, tm, d), lambda bi, j: (bi, j, 0)),
        out_shape=jax.ShapeDtypeStruct((b, t, d), F32),
        compiler_params=_params("parallel", "parallel"),
        name="out_proj",
    )(*parts, *weights, x3d, g.reshape(1, d))


FFN_HALO = 16


def _ffn_kernel(xh_ref, x_ref, gpre_ref, wg_ref, wv_ref, cw_ref, cb_ref, wd_ref, gpost_ref,
                y_ref, h_scr, acc_scr, *, tm, tiles_per_seq):
    i = pl.program_id(0)
    j = pl.program_id(1)
    nj = pl.num_programs(1)

    @pl.when(j == 0)
    def _():
        xe = jnp.concatenate([xh_ref[...], x_ref[...]], axis=0)
        h_scr[...] = _rms(xe, gpre_ref[...]).astype(BF16)

    h = h_scr[...]
    gate = jnp.dot(h, wg_ref[...], preferred_element_type=F32)
    val = jnp.dot(h[FFN_HALO:], wv_ref[...], preferred_element_type=F32)
    row = lax.broadcasted_iota(jnp.int32, gate.shape, 0)
    seq_start = (i % tiles_per_seq) == 0
    gate = jnp.where(jnp.logical_and(row < FFN_HALO, seq_start), 0.0, gate)
    cw = cw_ref[...]
    conv = cb_ref[...] + gate[FFN_HALO:] * cw[2:3]
    conv = conv + gate[FFN_HALO - 2:-2] * cw[0:1]
    conv = conv + gate[FFN_HALO - 1:-1] * cw[1:2]
    act = _gelu_tanh(conv) * val
    part = jnp.dot(act.astype(BF16), wd_ref[...], preferred_element_type=F32)

    @pl.when(j == 0)
    def _():
        acc_scr[...] = part

    @pl.when(j > 0)
    def _():
        acc_scr[...] += part

    @pl.when(j == nj - 1)
    def _():
        y_ref[...] = x_ref[...] + _rms(acc_scr[...], gpost_ref[...])


def _conv_ffn(x2d, seq_len, g_pre, w_up, conv_w, conv_b, w_down, g_post, tm=512, nf=2):
    n, d = x2d.shape
    tf = D_FF // nf
    hb = tm // FFN_HALO
    return pl.pallas_call(
        functools.partial(_ffn_kernel, tm=tm, tiles_per_seq=seq_len // tm),
        grid=(n // tm, nf),
        in_specs=[pl.BlockSpec((FFN_HALO, d), lambda i, j: (jnp.maximum(i * hb - 1, 0), 0)),
                  pl.BlockSpec((tm, d), lambda i, j: (i, 0)),
                  pl.BlockSpec((1, d), lambda i, j: (0, 0)),
                  pl.BlockSpec((d, tf), lambda i, j: (0, j)),
                  pl.BlockSpec((d, tf), lambda i, j: (0, nf + j)),
                  pl.BlockSpec((CONV_WIDTH, tf), lambda i, j: (0, j)),
                  pl.BlockSpec((1, tf), lambda i, j: (0, j)),
                  pl.BlockSpec((tf, d), lambda i, j: (j, 0)),
                  pl.BlockSpec((1, d), lambda i, j: (0, 0))],
        out_specs=pl.BlockSpec((tm, d), lambda i, j: (i, 0)),
        out_shape=jax.ShapeDtypeStruct((n, d), F32),
        scratch_shapes=[pltpu.VMEM((tm + FFN_HALO, d), BF16), pltpu.VMEM((tm, d), F32)],
        compiler_params=_params("parallel", "arbitrary"),
        name="conv_ffn",
    )(x2d, x2d, g_pre.reshape(1, d), w_up, w_up, conv_w, conv_b.reshape(1, D_FF), w_down,
      g_post.reshape(1, d))


def _chunk_phase(start, c):
    return ((start // ATT_CHUNK + c) % (POS_PERIOD // ATT_CHUNK)) * ATT_CHUNK


def _softmax_cols_chunks(parts, offs, extra=None):
    m = extra
    for part, off in zip(parts, offs):
        cm = jnp.max(part, axis=0, keepdims=True) + off
        m = cm if m is None else jnp.maximum(m, cm)
    m = jnp.where(m == NEG_INF, 0.0, m)
    p = jnp.concatenate([jnp.exp2(part - (m - off)) for part, off in zip(parts, offs)], axis=0)
    return p, m


def _swa_kernel(slope_ref, sink_ref, q_ref, s3_ref, k_ref, v_ref, o_ref, *, n_rep, window):
    g = pl.program_id(1)
    i = pl.program_id(2)
    t0 = i * TQ
    d = HEAD_DIM
    ck = ATT_CHUNK
    n_c = (window + TQ) // ck
    start = pl.multiple_of(jnp.maximum(t0 - window, 0), LANES)
    kb = k_ref[0, 0, pl.ds(start, n_c * ck), :]
    vb = v_ref[0, 0, :, pl.ds(start, n_c * ck)]
    t_row = t0 + lax.broadcasted_iota(jnp.int32, (1, TQ), 1)
    key_in_chunk = lax.broadcasted_iota(jnp.int32, (ck, TQ), 0)
    valid = []
    for c in range(n_c):
        dist = t_row - (start + c * ck + key_in_chunk)
        valid.append(jnp.logical_and(dist >= 0, dist < window))
    rel_t = (t_row - start).astype(F32)
    scores = []
    for r in range(n_rep):
        q_aug = jnp.concatenate([q_ref[0, 0, r], s3_ref[0, r]], axis=0)
        scores.append(jnp.dot(kb, q_aug, preferred_element_type=F32))
    probs, tails = [], []
    for r in range(n_rep):
        slope = slope_ref[g * n_rep + r]
        parts = [jnp.where(valid[c], scores[r][c * ck:(c + 1) * ck], NEG_INF) for c in range(n_c)]
        offs = [slope * (c * ck - _chunk_phase(start, c)).astype(F32) for c in range(n_c)]
        sink = sink_ref[g * n_rep + r] * LOG2E + slope * rel_t
        p, m = _softmax_cols_chunks(parts, offs, extra=sink)
        probs.append(p.astype(BF16))
        tails.append(jnp.exp2(sink - m))
    outs = []
    for r in range(n_rep):
        pv = jnp.dot(vb, probs[r], preferred_element_type=F32)
        outs.append(pv[0:d] / (pv[d:d + 1] + tails[r]))
    o_ref[0] = jnp.concatenate(outs, axis=0).astype(o_ref.dtype)


def _swa_attention(slopes, sinks, q_t, slope_rows, k, v_t):
    b, g, r, d, t = q_t.shape
    ka, va = k.shape[3], v_t.shape[2]
    smem = pl.BlockSpec(memory_space=pltpu.SMEM)
    return pl.pallas_call(
        functools.partial(_swa_kernel, n_rep=r, window=SWA_WINDOW),
        grid=(b, g, t // TQ),
        in_specs=[smem, smem,
                  pl.BlockSpec((1, 1, r, d, TQ), lambda bi, gi, i: (bi, gi, 0, 0, i)),
                  pl.BlockSpec((1, r, ka - d, TQ), lambda bi, gi, i: (gi, 0, 0, 0)),
                  pl.BlockSpec((1, 1, t, ka), lambda bi, gi, i: (bi, gi, 0, 0)),
                  pl.BlockSpec((1, 1, va, t), lambda bi, gi, i: (bi, gi, 0, 0))],
        out_specs=pl.BlockSpec((1, r * d, TQ), lambda bi, gi, i: (bi, gi, i)),
        out_shape=jax.ShapeDtypeStruct((b, g * r * d, t), BF16),
        compiler_params=_params("parallel", "parallel", "parallel"),
        name="swa_attention",
    )(slopes, sinks, q_t, slope_rows, k, v_t)


def _rwkv_kernel(zr_ref, zk_ref, zv_ref, zl_ref, mur_ref, muk_ref, muv_ref, mul_ref,
                 w0_ref, a0_ref, kk_ref, ka_ref, rk_ref, lng_ref, lnb_ref,
                 w2_ref, a2_ref, g2_ref, o_ref, prev_scr, s_scr):
    c = pl.program_id(2)
    cs = RWKV_CHUNK
    wd = RWKV_QUAD * HEAD_DIM

    @pl.when(c == 0)
    def _():
        prev_scr[...] = jnp.zeros_like(prev_scr)
        s_scr[...] = jnp.zeros_like(s_scr)

    row = lax.broadcasted_iota(jnp.int32, (cs, wd), 0)

    def shifted(z_ref, mu_ref, idx):
        z = z_ref[0]
        prev = prev_scr[idx, 0:1, :]
        zp = jnp.where(row == 0, prev, pltpu.roll(z, 1, 0))
        prev_scr[idx, 0:1, :] = z[cs - 1:cs, :]
        return z + (zp - z) * mu_ref[...]

    r = shifted(zr_ref, mur_ref, 0)
    k = shifted(zk_ref, muk_ref, 1)
    v = shifted(zv_ref, muv_ref, 2)
    lo = shifted(zl_ref, mul_ref, 3)

    yw = w0_ref[...] + _mm(jnp.tanh(lo), w2_ref[...])
    ya = a0_ref[...] + _mm(lo, a2_ref[...])
    gate = _mm(_sigmoid(lo), g2_ref[...])
    softplus = jnp.maximum(-yw, 0.0) + jnp.log(1.0 + jnp.exp(-jnp.abs(yw)))
    logd = -jnp.exp(-softplus - 0.5)
    a = _sigmoid(ya)

    i0 = lax.broadcasted_iota(jnp.int32, (wd, wd), 0)
    i1 = lax.broadcasted_iota(jnp.int32, (wd, wd), 1)
    same_head = (i0 // HEAD_DIM) == (i1 // HEAD_DIM)
    ones_head = jnp.where(same_head, 1.0, 0.0).astype(BF16)

    def head_sum(x):
        return _mm_exact_rhs(x, ones_head, 2)

    kk = k * kk_ref[...]
    kk = kk / jnp.maximum(jnp.sqrt(head_sum(kk * kk)), 1e-12)
    kmod = k * (1.0 + (a - 1.0) * ka_ref[...])
    b = kk * a

    c0 = lax.broadcasted_iota(jnp.int32, (cs, cs), 0)
    c1 = lax.broadcasted_iota(jnp.int32, (cs, cs), 1)
    tri = jnp.where(c0 >= c1, 1.0, 0.0).astype(BF16)
    cl = _mm_exact_lhs(tri, logd, 3)
    cl_end = cl[cs - 1:cs, :]
    w_inc = jnp.exp(cl)
    w_exc = jnp.exp(cl - logd)
    w_inv = jnp.exp(-cl)
    w_rem = jnp.exp(cl_end - cl)

    def stack(x):
        return jnp.where(same_head, jnp.concatenate([x] * RWKV_QUAD, axis=0), 0.0)

    x_kk = stack(kk * w_exc)
    x_r = stack(r * w_inc)
    x_b = stack(b * w_inv)
    x_k = stack(kmod * w_inv)
    x_v = stack(v)
    x_bd = stack(b * w_rem)
    x_kd = stack(kmod * w_rem)

    strict = (i0 % cs) > (i1 % cs)
    incl = (i0 % cs) >= (i1 % cs)
    a_ub = jnp.where(strict, _mm_nt(x_kk, x_b), 0.0)
    a_uk = jnp.where(strict, _mm_nt(x_kk, x_k), 0.0)
    a_rb = jnp.where(incl, _mm_nt(x_r, x_b), 0.0)
    a_rk = jnp.where(incl, _mm_nt(x_r, x_k), 0.0)

    eye = i0 == i1
    pw = a_ub
    t_inv = jnp.where(eye, 1.0, 0.0) - a_ub
    for _ in range(int(math.log2(cs)) - 1):
        pw = _mm(pw, pw)
        t_inv = t_inv + _mm(t_inv, pw)

    p1 = _mm(t_inv, x_kk)
    p2 = _mm(t_inv, _mm(a_uk, x_v))
    q1 = x_r - _mm(a_rb, p1)
    q2 = _mm(a_rk, x_v) - _mm(a_rb, p2)
    g_mat = jnp.where(eye, jnp.exp(cl_end), 0.0) - _mm_tn(x_bd, p1)
    h_mat = _mm_tn(x_kd, x_v) - _mm_tn(x_bd, p2)

    s0 = s_scr[...]
    ys = _mm(q1, s0) + q2
    s_scr[...] = _mm(g_mat, s0) + h_mat
    y = ys[0:cs]
    for hd in range(1, RWKV_QUAD):
        y = y + ys[hd * cs:(hd + 1) * cs]

    inv_n = 1.0 / HEAD_DIM
    mean = head_sum(y) * inv_n
    yc = y - mean
    var = head_sum(yc * yc) * inv_n
    yn = yc * lax.rsqrt(var + RWKV_GN_EPS) * lng_ref[...] + lnb_ref[...]
    bonus = head_sum(r * kmod * rk_ref[...]) * v
    o_ref[0] = (yn + bonus) * gate


def _rwkv_time_mix(z3, mu, w0, w2, a0, a2, g2, k_k, k_a, r_k, ln_g, ln_b):
    b, t, _ = z3.shape
    wd = RWKV_QUAD * HEAD_DIM
    nq = RWKV_DIM // wd
    base = SWA_COLS // wd
    lora_rows = RWKV_W_LORA + RWKV_A_LORA + RWKV_G_LORA
    assert lora_rows == wd and SWA_COLS % wd == 0

    def pad_rows(w, start):
        return jnp.zeros((wd, RWKV_DIM), F32).at[start:start + w.shape[0]].set(w).astype(BF16)

    w2p = pad_rows(w2, 0)
    a2p = pad_rows(a2, RWKV_W_LORA)
    g2p = pad_rows(g2, RWKV_W_LORA + RWKV_A_LORA)
    mu2 = mu.reshape(1, -1)
    row = lambda x: x.reshape(1, RWKV_DIM)
    zspec = lambda off: pl.BlockSpec((1, RWKV_CHUNK, wd), lambda bi, qi, ci: (bi, ci, base + off + qi))
    zlspec = pl.BlockSpec((1, RWKV_CHUNK, wd), lambda bi, qi, ci: (bi, ci, base + 3 * nq))
    muspec = lambda off: pl.BlockSpec((1, wd), lambda bi, qi, ci: (0, off + qi))
    mulspec = pl.BlockSpec((1, wd), lambda bi, qi, ci: (0, 3 * nq))
    pspec = pl.BlockSpec((1, wd), lambda bi, qi, ci: (0, qi))
    wspec = pl.BlockSpec((wd, wd), lambda bi, qi, ci: (0, qi))
    return pl.pallas_call(
        _rwkv_kernel,
        grid=(b, nq, t // RWKV_CHUNK),
        in_specs=[zspec(0), zspec(nq), zspec(2 * nq), zlspec,
                  muspec(0), muspec(nq), muspec(2 * nq), mulspec,
                  pspec, pspec, pspec, pspec, pspec, pspec, pspec,
                  wspec, wspec, wspec],
        out_specs=pl.BlockSpec((1, RWKV_CHUNK, wd), lambda bi, qi, ci: (bi, ci, qi)),
        out_shape=jax.ShapeDtypeStruct((b, t, RWKV_DIM), F32),
        scratch_shapes=[pltpu.VMEM((4, 8, wd), F32), pltpu.VMEM((wd, wd), F32)],
        compiler_params=_params("parallel", "parallel", "arbitrary"),
        name="rwkv7_time_mix",
    )(z3, z3, z3, z3, mu2, mu2, mu2, mu2,
      row(w0), row(a0), row(k_k), row(k_a), row(r_k), row(ln_g), row(ln_b),
      w2p, a2p, g2p)


def _compress_kernel(kb_ref, vb_ref, posk_ref, posv_ref, w1k_ref, w1v_ref, w2k_ref, w2v_ref,
                     kc_ref, vc_ref, *, n_cmp):
    half = CMP_STRIDE * HEAD_DIM

    def run(blk_ref, pos_ref, w1_ref, w2_ref):
        a = blk_ref[0, 0]
        lo = _mm(a + pos_ref[0:1, :], w1_ref[0:half, :])
        hi = _mm(a + pos_ref[1:2, :], w1_ref[half:2 * half, :])
        n_half = a.shape[0]
        hid = lo + pltpu.roll(hi, n_half - 1, 0)
        out = _mm(_gelu_tanh(hid), w2_ref[...])
        rows = lax.broadcasted_iota(jnp.int32, out.shape, 0)
        return jnp.where(rows < n_cmp, out, 0.0)

    kc_ref[0, 0] = run(kb_ref, posk_ref, w1k_ref, w2k_ref).astype(BF16)
    vc_ref[0, 0] = run(vb_ref, posv_ref, w1v_ref, w2v_ref).T.astype(BF16)


def _compress(kc_blocks, vc_blocks, pos_k, pos_v, w1_k, w1_v, w2_k, w2_v, n_cmp):
    b, g, nh, hw = kc_blocks.shape
    blk = pl.BlockSpec((1, 1, nh, hw), lambda bi, gi: (bi, gi, 0, 0))
    full = lambda a: pl.BlockSpec(a.shape, lambda bi, gi: (0,) * a.ndim)
    return pl.pallas_call(
        functools.partial(_compress_kernel, n_cmp=n_cmp),
        grid=(b, g),
        in_specs=[blk, blk, full(pos_k), full(pos_v), full(w1_k), full(w1_v), full(w2_k), full(w2_v)],
        out_specs=[pl.BlockSpec((1, 1, nh, HEAD_DIM), lambda bi, gi: (bi, gi, 0, 0)),
                   pl.BlockSpec((1, 1, HEAD_DIM, nh), lambda bi, gi: (bi, gi, 0, 0))],
        out_shape=[jax.ShapeDtypeStruct((b, g, nh, HEAD_DIM), BF16),
                   jax.ShapeDtypeStruct((b, g, HEAD_DIM, nh), BF16)],
        compiler_params=_params("parallel", "parallel"),
        name="nsa_compress",
    )(kc_blocks, vc_blocks, pos_k, pos_v, w1_k, w1_v, w2_k, w2_v)


def _nsa_kernel(slope_ref, q_ref, s3_ref, kc_ref, vc_ref, ks_ref, vs_ref, kw_ref, vw_ref, gl_ref, ovw_ref,
                o_ref, sel_scr, list_scr, valid_scr, sa_scr, sb_scr, m_scr, acc_scr, *, n_rep, n_blk):
    g = pl.program_id(1)
    i = pl.program_id(2)
    t0 = i * TQ
    d = HEAD_DIM
    ck = ATT_CHUNK
    gsz = SEL_GROUP
    bpg = gsz // SEL_LEN
    n_grp = n_blk // bpg
    q_aug = [jnp.concatenate([q_ref[0, 0, r], s3_ref[0, r]], axis=0) for r in range(n_rep)]
    slopes = [slope_ref[g * n_rep + r] for r in range(n_rep)]
    t_row = t0 + lax.broadcasted_iota(jnp.int32, (1, TQ), 1)
    key_in_chunk = lax.broadcasted_iota(jnp.int32, (ck, TQ), 0)
    key_in_blk = lax.broadcasted_iota(jnp.int32, (SEL_LEN, TQ), 0)

    kc = kc_ref[0, 0]
    vc = vc_ref[0, 0]
    s_cmp = [jnp.dot(kc, q_aug[r], preferred_element_type=F32) for r in range(n_rep)]
    n_wc = (NSA_WINDOW + TQ) // ck
    start = pl.multiple_of(jnp.maximum(t0 - NSA_WINDOW, 0), LANES)
    kw_band = kw_ref[0, 0, pl.ds(start, n_wc * ck), :]
    vw_band = vw_ref[0, 0, :, pl.ds(start, n_wc * ck)]
    s_win = [jnp.dot(kw_band, q_aug[r], preferred_element_type=F32) for r in range(n_rep)]

    def score_group(gidx, s_scr):
        k_cat = ks_ref[0, 0, pl.ds(pl.multiple_of(gidx * gsz, gsz), gsz), :]
        for r in range(n_rep):
            s_scr[r] = jnp.dot(k_cat, q_aug[r], preferred_element_type=F32)

    diag_group = (t0 + TQ - 1) // gsz
    score_group(diag_group, sb_scr)

    cmp_end = lax.broadcasted_iota(jnp.int32, (kc.shape[0], TQ), 0) * CMP_STRIDE + (CMP_LEN - 1)
    valid_c = cmp_end <= t_row
    p_sum = None
    o_cmp = []
    for r in range(n_rep):
        s = jnp.where(valid_c, s_cmp[r], NEG_INF)
        m = jnp.max(s, axis=0, keepdims=True)
        m = jnp.where(m == NEG_INF, 0.0, m)
        p = jnp.exp2(s - m)
        den = jnp.sum(p, axis=0, keepdims=True)
        p = p * (1.0 / jnp.where(den > 0, den, 1.0))
        p_sum = p if p_sum is None else p_sum + p
        o_cmp.append(jnp.dot(vc, p.astype(BF16), preferred_element_type=F32))

    o_win = []
    win_valid = []
    for c in range(n_wc):
        dist = t_row - (start + c * ck + key_in_chunk)
        win_valid.append(jnp.logical_and(dist >= 0, dist < NSA_WINDOW))
    for r in range(n_rep):
        parts = [jnp.where(win_valid[c], s_win[r][c * ck:(c + 1) * ck], NEG_INF) for c in range(n_wc)]
        offs = [slopes[r] * (c * ck - _chunk_phase(start, c)).astype(F32) for c in range(n_wc)]
        p, _ = _softmax_cols_chunks(parts, offs)
        pv = jnp.dot(vw_band, p.astype(BF16), preferred_element_type=F32)
        den = pv[d:d + 1]
        o_win.append(pv[0:d] / jnp.where(den > 0, den, 1.0))

    imp = _mm_exact_lhs(ovw_ref[...], p_sum, 3)
    blk = lax.broadcasted_iota(jnp.int32, (n_blk, TQ), 0)
    cur = (t0 + lax.broadcasted_iota(jnp.int32, (n_blk, TQ), 1)) // SEL_LEN
    forced = jnp.logical_or(blk == 0, jnp.logical_or(blk == cur, blk == cur - 1))
    score = jnp.where(forced, jnp.inf, jnp.where(blk > cur, NEG_INF, imp))
    sel = jnp.zeros((n_blk, TQ), F32)
    for _ in range(min(N_SEL, n_blk)):
        best = jnp.max(score, axis=0, keepdims=True)
        first = jnp.min(jnp.where(score == best, blk, n_blk), axis=0, keepdims=True)
        hit = blk == first
        sel = jnp.where(hit, 1.0, sel)
        score = jnp.where(hit, NEG_INF, score)
    sel_scr[...] = sel

    ones = jnp.ones((8, TQ), BF16)
    cnt = _mm_nt(ones, sel)
    pool = (lax.broadcasted_iota(jnp.int32, (n_blk, LANES), 0) // bpg
            == lax.broadcasted_iota(jnp.int32, (n_blk, LANES), 1))
    cnt_g = _mm(cnt, jnp.where(pool, 1.0, 0.0))
    for j in range(n_grp + 1):
        list_scr[j] = 0
        valid_scr[j] = 0
    pos = jnp.int32(n_grp)
    for j in reversed(range(n_grp)):
        hit = jnp.logical_and(cnt_g[0, j] > 0.0, j < diag_group)
        pos = pos - hit.astype(jnp.int32)
        slot = jnp.where(hit, pos, n_grp)
        list_scr[slot] = j
        valid_scr[slot] = 1
    n_pairs = (n_grp - pos + 1) // 2
    first = n_grp - 2 * n_pairs

    m_scr[...] = jnp.full(m_scr.shape, NEG_INF, F32)
    acc_scr[...] = jnp.zeros(acc_scr.shape, F32)

    def reduce_group(gidx, live, s_scr, diagonal):
        k0 = pl.multiple_of(gidx * gsz, gsz)
        rows = sel_scr[pl.ds(pl.multiple_of(gidx * bpg, bpg), bpg), :] * live
        v_cat = vs_ref[0, 0, :, pl.ds(k0, gsz)]
        masks = []
        for u in range(bpg):
            picked = rows[u:u + 1, :] > 0.5
            if diagonal:
                picked = jnp.logical_and(picked, k0 + u * SEL_LEN + key_in_blk <= t_row)
            masks.append(picked)
        rel = (k0 - t0).astype(F32)
        for r in range(n_rep):
            parts = [jnp.where(masks[u], s_scr[r, pl.ds(u * SEL_LEN, SEL_LEN), :], NEG_INF) for u in range(bpg)]
            off = slopes[r] * rel
            top = parts[0]
            for part in parts[1:]:
                top = jnp.maximum(top, part)
            m_old = m_scr[r]
            m_new = jnp.maximum(m_old, jnp.max(top, axis=0, keepdims=True) + off)
            m_safe = jnp.where(m_new == NEG_INF, 0.0, m_new)
            alpha = jnp.exp2(m_old - m_safe)
            sub = m_safe - off
            p = jnp.concatenate([jnp.exp2(part - sub) for part in parts], axis=0)
            acc_scr[r] = alpha * acc_scr[r] + jnp.dot(v_cat, p.astype(BF16), preferred_element_type=F32)
            m_scr[r] = m_new

    score_group(list_scr[first], sa_scr)
    reduce_group(diag_group, 1.0, sb_scr, True)

    def body(pi, carry):
        a = first + 2 * pi
        score_group(list_scr[a + 1], sb_scr)
        reduce_group(list_scr[a], valid_scr[a].astype(F32), sa_scr, False)
        score_group(list_scr[a + 2], sa_scr)
        reduce_group(list_scr[a + 1], valid_scr[a + 1].astype(F32), sb_scr, False)
        return carry

    lax.fori_loop(0, n_pairs, body, 0)

    gates = _sigmoid(gl_ref[0, 0])
    outs = []
    for r in range(n_rep):
        acc = acc_scr[r]
        l_fin = acc[d:d + 1]
        o_slc = acc[0:d] / jnp.where(l_fin > 0, l_fin, 1.0)
        g0 = gates[N_BRANCH * r + 0:N_BRANCH * r + 1, :]
        g1 = gates[N_BRANCH * r + 1:N_BRANCH * r + 2, :]
        g2 = gates[N_BRANCH * r + 2:N_BRANCH * r + 3, :]
        outs.append(g0 * o_cmp[r] + g1 * o_slc + g2 * o_win[r])
    o_ref[0] = jnp.concatenate(outs, axis=0).astype(o_ref.dtype)


def _nsa_attention(slopes, q_t, slope_rows, kc, vc_t, ks, vs_t, kw, vw_t, gl_t, ov_w):
    b, g, r, d, t = q_t.shape
    n_blk = t // SEL_LEN
    n_grp = t // SEL_GROUP
    ncp = kc.shape[2]
    ka = kc.shape[3]
    va = vs_t.shape[2]
    idx4 = lambda bi, gi, i: (bi, gi, 0, 0)
    return pl.pallas_call(
        functools.partial(_nsa_kernel, n_rep=r, n_blk=n_blk),
        grid=(b, g, t // TQ),
        in_specs=[pl.BlockSpec(memory_space=pltpu.SMEM),
                  pl.BlockSpec((1, 1, r, d, TQ), lambda bi, gi, i: (bi, gi, 0, 0, i)),
                  pl.BlockSpec((1, r, ka - d, TQ), lambda bi, gi, i: (gi, 0, 0, 0)),
                  pl.BlockSpec((1, 1, ncp, ka), idx4),
                  pl.BlockSpec((1, 1, d, ncp), idx4),
                  pl.BlockSpec((1, 1, t, ka), idx4),
                  pl.BlockSpec((1, 1, va, t), idx4),
                  pl.BlockSpec((1, 1, t, ka), idx4),
                  pl.BlockSpec((1, 1, va, t), idx4),
                  pl.BlockSpec((1, 1, r * N_BRANCH, TQ), lambda bi, gi, i: (bi, gi, 0, i)),
                  pl.BlockSpec(ov_w.shape, lambda bi, gi, i: (0, 0))],
        out_specs=pl.BlockSpec((1, r * d, TQ), lambda bi, gi, i: (bi, gi, i)),
        out_shape=jax.ShapeDtypeStruct((b, g * r * d, t), BF16),
        scratch_shapes=[pltpu.VMEM((n_blk, TQ), F32),
                        pltpu.SMEM((n_grp + 2,), jnp.int32),
                        pltpu.SMEM((n_grp + 2,), jnp.int32),
                        pltpu.VMEM((r, SEL_GROUP, TQ), F32),
                        pltpu.VMEM((r, SEL_GROUP, TQ), F32),
                        pltpu.VMEM((r, 1, TQ), F32),
                        pltpu.VMEM((r, va, TQ), F32)],
        compiler_params=_params("parallel", "parallel", "arbitrary"),
        name="nsa_attention",
    )(slopes, q_t, slope_rows, kc, vc_t, ks, vs_t, kw, vw_t, gl_t, ov_w)


def _alibi_slopes(n_heads):
    return jnp.exp2(-8.0 * jnp.arange(1, n_heads + 1, dtype=F32) / n_heads) * LOG2E


def _truncation_split_bf16(x, n):
    parts = []
    for _ in range(n):
        top = lax.bitcast_convert_type(lax.bitcast_convert_type(x, jnp.uint32) & jnp.uint32(0xFFFF0000), F32)
        parts.append(top.astype(BF16))
        x = x - top
    return parts


def _slope_query_rows(n_heads, n_groups):
    parts = _truncation_split_bf16(_alibi_slopes(n_heads), 3)
    rows = jnp.zeros((n_heads, KEY_AUG), BF16)
    for j, p in enumerate(parts):
        rows = rows.at[:, j].set(p).at[:, 3 + j].set(p)
    rows = jnp.broadcast_to(rows[:, :, None], (n_heads, KEY_AUG, TQ))
    return rows.reshape(n_groups, n_heads // n_groups, KEY_AUG, TQ)


def _augment_keys(k_tm, lo, hi):
    b, g, n, _ = k_tm.shape
    extra = jnp.zeros((n, KEY_AUG), F32)
    extra = extra.at[:, 0:3].set(lo.astype(F32)[:, None]).at[:, 3:6].set(hi.astype(F32)[:, None]).astype(BF16)
    return jnp.concatenate([k_tm, jnp.broadcast_to(extra, (b, g, n, KEY_AUG))], axis=-1)


def _token_key_positions(t):
    pos = jnp.arange(t) % POS_PERIOD
    lo = pos % ATT_CHUNK
    return lo, pos - lo


def _augment_values(v_fm):
    b, g, _, n = v_fm.shape
    extra = jnp.zeros((VAL_AUG, n), BF16).at[0].set(1.0)
    return jnp.concatenate([v_fm, jnp.broadcast_to(extra, (b, g, VAL_AUG, n))], axis=2)


def _heads_feature_major(a, b, t, g):
    return a.reshape(b, t, g, HEAD_DIM).transpose(0, 2, 3, 1)


def _heads_token_major(a, b, t, g):
    return a.reshape(b, t, g, HEAD_DIM).transpose(0, 2, 1, 3)


def _hybrid_layer(x3d, g_pre, g_post, w_in, w_out, sinks, mu, w0, w2, a0, a2, g2, k_k, k_a,
                  r_k, ln_g, ln_b):
    b, t, d = x3d.shape
    z = _norm_proj(x3d.reshape(b * t, d), g_pre, w_in.astype(BF16))
    gq, rq = SWA_KV_HEADS, SWA_HEADS // SWA_KV_HEADS
    scale = HEAD_DIM ** -0.5 * LOG2E
    lo, hi = _token_key_positions(t)
    q_t = (z[:, :SWA_Q_COLS] * scale).astype(BF16).reshape(b, t, gq, rq, HEAD_DIM).transpose(0, 2, 3, 4, 1)
    k = _augment_keys(_heads_token_major(z[:, SWA_Q_COLS:SWA_Q_COLS + SWA_KV_COLS].astype(BF16), b, t, gq), lo, hi)
    v_t = _augment_values(_heads_feature_major(z[:, SWA_Q_COLS + SWA_KV_COLS:SWA_COLS].astype(BF16), b, t, gq))
    o_a = _swa_attention(_alibi_slopes(SWA_HEADS), sinks.astype(F32), q_t, _slope_query_rows(SWA_HEADS, gq), k, v_t)
    o_b = _rwkv_time_mix(z.reshape(b, t, -1), mu, w0, w2, a0, a2, g2, k_k, k_a, r_k.reshape(-1), ln_g, ln_b)
    w_out = w_out.astype(BF16)
    return _out_proj([o_a, o_b], [True, False], [w_out[:SWA_Q_COLS], w_out[SWA_Q_COLS:]], x3d, g_post)


def _overlap_weights(n_blk, n_cmp_pad, n_cmp):
    ratio, span = SEL_LEN // CMP_STRIDE, CMP_LEN // CMP_STRIDE
    w = np.zeros((n_blk, n_cmp_pad), np.float32)
    for j in range(n_blk):
        for a in range(ratio):
            for s in range(span):
                c = ratio * j + a - s
                if 0 <= c < n_cmp:
                    w[j, c] += 1.0
    return jnp.asarray(w, BF16)


def _nsa_layer(x3d, g_pre, g_post, w_in, w_out, pos_k, w1_k, w2_k, pos_v, w1_v, w2_v):
    b, t, d = x3d.shape
    g, r = NSA_KV_HEADS, NSA_HEADS // NSA_KV_HEADS
    n_cols = w_in.shape[1]
    pad = (-n_cols) % LANES
    w_in = jnp.pad(w_in.astype(BF16), ((0, 0), (0, pad)))
    z = _norm_proj(x3d.reshape(b * t, d), g_pre, w_in)
    off = NSA_Q_COLS
    cols = lambda j: z[:, off + j * NSA_KV_COLS:off + (j + 1) * NSA_KV_COLS]
    scale = HEAD_DIM ** -0.5 * LOG2E
    q_t = (z[:, :NSA_Q_COLS] * scale).astype(BF16).reshape(b, t, g, r, HEAD_DIM).transpose(0, 2, 3, 4, 1)
    n_half = t // CMP_STRIDE
    n_cmp = (t - CMP_LEN) // CMP_STRIDE + 1
    half_rows = lambda a: a.reshape(b, n_half, CMP_STRIDE, g, HEAD_DIM).transpose(0, 3, 1, 2, 4).reshape(
        b, g, n_half, CMP_STRIDE * HEAD_DIM)
    flat_pos = lambda p: p.reshape(CMP_LEN // CMP_STRIDE, CMP_STRIDE * HEAD_DIM)
    flat_w1 = lambda w: w.reshape(CMP_LEN * HEAD_DIM, CMP_HIDDEN).astype(BF16)
    kc, vc_t = _compress(half_rows(cols(0)), half_rows(cols(1)), flat_pos(pos_k), flat_pos(pos_v),
                         flat_w1(w1_k), flat_w1(w1_v), w2_k.astype(BF16), w2_v.astype(BF16), n_cmp)
    lo, hi = _token_key_positions(t)
    cmp_pos = jnp.arange(n_half) * CMP_STRIDE
    kc = _augment_keys(kc, cmp_pos, jnp.zeros_like(cmp_pos))
    ks = _augment_keys(_heads_token_major(cols(2).astype(BF16), b, t, g), lo, hi)
    vs_t = _augment_values(_heads_feature_major(cols(3).astype(BF16), b, t, g))
    kw = _augment_keys(_heads_token_major(cols(4).astype(BF16), b, t, g), lo, hi)
    vw_t = _augment_values(_heads_feature_major(cols(5).astype(BF16), b, t, g))
    gl0 = off + 6 * NSA_KV_COLS
    gl_t = z[:, gl0:gl0 + NSA_HEADS * N_BRANCH].reshape(b, t, g, r * N_BRANCH).transpose(0, 2, 3, 1)
    ov_w = _overlap_weights(t // SEL_LEN, n_half, n_cmp)
    o = _nsa_attention(_alibi_slopes(NSA_HEADS), q_t, _slope_query_rows(NSA_HEADS, g), kc, vc_t, ks, vs_t,
                       kw, vw_t, gl_t, ov_w)
    return _out_proj([o], [True], [w_out.astype(BF16)], x3d, g_post)


def kernel(x, mix_pre_g, mix_post_g, ffn_pre_g, ffn_post_g, hy_w_in, hy_w_out, swa_sinks, rwkv_mu, rwkv_w0, rwkv_w2, rwkv_a0, rwkv_a2, rwkv_g2, rwkv_k_k, rwkv_k_a, rwkv_r_k, rwkv_ln_g, rwkv_ln_b, nsa_w_in, nsa_w_out, nsa_cmp_pos_k, nsa_cmp_w1_k, nsa_cmp_w2_k, nsa_cmp_pos_v, nsa_cmp_w1_v, nsa_cmp_w2_v, ffn_w_up, ffn_conv_w, ffn_conv_b, ffn_w_down):
    b, t, d = x.shape
    depth = mix_pre_g.shape[0]
    for layer in range(depth):
        i = layer // 2
        if layer % 2 == 0:
            x = _hybrid_layer(x, mix_pre_g[layer], mix_post_g[layer], hy_w_in[i], hy_w_out[i],
                              swa_sinks[i], rwkv_mu[i], rwkv_w0[i], rwkv_w2[i], rwkv_a0[i], rwkv_a2[i],
                              rwkv_g2[i], rwkv_k_k[i], rwkv_k_a[i], rwkv_r_k[i], rwkv_ln_g[i], rwkv_ln_b[i])
        else:
            x = _nsa_layer(x, mix_pre_g[layer], mix_post_g[layer], nsa_w_in[i], nsa_w_out[i],
                           nsa_cmp_pos_k[i], nsa_cmp_w1_k[i], nsa_cmp_w2_k[i],
                           nsa_cmp_pos_v[i], nsa_cmp_w1_v[i], nsa_cmp_w2_v[i])
        x = _conv_ffn(x.reshape(b * t, d), t, ffn_pre_g[layer], ffn_w_up[layer].astype(BF16), ffn_conv_w[layer],
                      ffn_conv_b[layer], ffn_w_down[layer].astype(BF16), ffn_post_g[layer]).reshape(b, t, d)
    return x
```

```python
import functools
import math

import numpy as np
import jax
import jax.numpy as jnp
from jax import lax
from jax.experimental import pallas as pl
from jax.experimental.pallas import tpu as pltpu

F32 = jnp.float32
BF16 = jnp.bfloat16

D_MODEL = 1024
HEAD_DIM = 64
SWA_HEADS = 8
SWA_KV_HEADS = 2
SWA_WINDOW = 128
RWKV_HEADS = 8
RWKV_DIM = RWKV_HEADS * HEAD_DIM
RWKV_W_LORA = 64
RWKV_A_LORA = 64
RWKV_G_LORA = 128
RWKV_GN_EPS = 64e-5
NSA_HEADS = 16
NSA_KV_HEADS = 4
CMP_LEN = 32
CMP_STRIDE = 16
CMP_HIDDEN = 256
SEL_LEN = 64
N_SEL = 8
NSA_WINDOW = 256
N_BRANCH = 3
D_FF = 2816
CONV_WIDTH = 3
NORM_EPS = 1e-6

SWA_Q_COLS = SWA_HEADS * HEAD_DIM
SWA_KV_COLS = SWA_KV_HEADS * HEAD_DIM
SWA_COLS = SWA_Q_COLS + 2 * SWA_KV_COLS
NSA_KV_COLS = NSA_KV_HEADS * HEAD_DIM
NSA_Q_COLS = NSA_HEADS * HEAD_DIM

LANES = 128
VMEM_LIMIT = 56 * 1024 * 1024

TQ = 256
ATT_CHUNK = 128
POS_PERIOD = 512
KEY_AUG = 16
VAL_AUG = 16
SEL_GROUP = POS_PERIOD
RWKV_CHUNK = 64
RWKV_QUAD = 4
RWKV_BATCH = 4
NEG_INF = float("-inf")
LOG2E = math.log2(math.e)


def _params(*sem):
    return pltpu.CompilerParams(dimension_semantics=sem, vmem_limit_bytes=VMEM_LIMIT)


def _rms(x, g):
    return x * lax.rsqrt(jnp.mean(x * x, axis=-1, keepdims=True) + NORM_EPS) * g


def _mm(a, b):
    return jnp.dot(a.astype(BF16), b.astype(BF16), preferred_element_type=F32)


def _mm_nt(a, b):
    return lax.dot_general(a.astype(BF16), b.astype(BF16), (((1,), (1,)), ((), ())),
                           preferred_element_type=F32)


def _mm_tn(a, b):
    return lax.dot_general(a.astype(BF16), b.astype(BF16), (((0,), (0,)), ((), ())),
                           preferred_element_type=F32)


def _split_bf16(x, n):
    parts = []
    for _ in range(n):
        p = x.astype(BF16)
        parts.append(p)
        x = x - p.astype(F32)
    return parts


def _mm_exact_lhs(a_bf16, x, n):
    out = None
    for p in _split_bf16(x, n):
        t = jnp.dot(a_bf16, p, preferred_element_type=F32)
        out = t if out is None else out + t
    return out


def _mm_exact_rhs(x, b_bf16, n):
    out = None
    for p in _split_bf16(x, n):
        t = jnp.dot(p, b_bf16, preferred_element_type=F32)
        out = t if out is None else out + t
    return out


def _sigmoid(x):
    return 1.0 / (1.0 + jnp.exp(-x))


def _gelu_tanh(x):
    c = math.sqrt(2.0 / math.pi)
    return 0.5 * x * (1.0 + jnp.tanh(c * (x + 0.044715 * (x * x * x))))


def _norm_proj_kernel(x_ref, g_ref, w_ref, o_ref):
    h = _rms(x_ref[...], g_ref[...])
    o_ref[...] = jnp.dot(h.astype(BF16), w_ref[...], preferred_element_type=F32)


def _norm_proj(x2d, g, w, tm=512):
    n, d = x2d.shape
    c = w.shape[1]
    return pl.pallas_call(
        _norm_proj_kernel,
        grid=(n // tm,),
        in_specs=[pl.BlockSpec((tm, d), lambda i: (i, 0)),
                  pl.BlockSpec((1, d), lambda i: (0, 0)),
                  pl.BlockSpec((d, c), lambda i: (0, 0))],
        out_specs=pl.BlockSpec((tm, c), lambda i: (i, 0)),
        out_shape=jax.ShapeDtypeStruct((n, c), F32),
        compiler_params=_params("parallel"),
        name="norm_proj",
    )(x2d, g.reshape(1, d), w)


def _out_proj_kernel(*refs, feature_major):
    n_parts = len(feature_major)
    o_refs = refs[:n_parts]
    w_refs = refs[n_parts:2 * n_parts]
    x_ref, g_ref, y_ref = refs[2 * n_parts:]
    m = None
    for o_ref, w_ref, fm in zip(o_refs, w_refs, feature_major):
        if fm:
            t = lax.dot_general(o_ref[0].astype(BF16), w_ref[...], (((0,), (0,)), ((), ())),
                                preferred_element_type=F32)
        else:
            t = jnp.dot(o_ref[0].astype(BF16), w_ref[...], preferred_element_type=F32)
        m = t if m is None else m + t
    y_ref[0] = x_ref[0] + _rms(m, g_ref[...])


def _out_proj(parts, feature_major, weights, x3d, g, tm=512):
    b, t, d = x3d.shape
    in_specs = []
    for p, fm in zip(parts, feature_major):
        if fm:
            in_specs.append(pl.BlockSpec((1, p.shape[1], tm), lambda bi, j: (bi, 0, j)))
        else:
            in_specs.append(pl.BlockSpec((1, tm, p.shape[2]), lambda bi, j: (bi, j, 0)))
    in_specs += [pl.BlockSpec(w.shape, lambda bi, j: (0, 0)) for w in weights]
    in_specs += [pl.BlockSpec((1, tm, d), lambda bi, j: (bi, j, 0)), pl.BlockSpec((1, d), lambda bi, j: (0, 0))]
    return pl.pallas_call(
        functools.partial(_out_proj_kernel, feature_major=tuple(feature_major)),
        grid=(b, t // tm),
        in_specs=in_specs,
        out_specs=pl.BlockSpec((1, tm, d), lambda bi, j: (bi, j, 0)),
        out_shape=jax.ShapeDtypeStruct((b, t, d), F32),
        compiler_params=_params("parallel", "parallel"),
        name="out_proj",
    )(*parts, *weights, x3d, g.reshape(1, d))


FFN_HALO = 16


def _ffn_kernel(xh_ref, x_ref, gpre_ref, wg_ref, wv_ref, cw_ref, cb_ref, wd_ref, gpost_ref,
                y_ref, h_scr, acc_scr, *, tm, tiles_per_seq):
    i = pl.program_id(0)
    j = pl.program_id(1)
    nj = pl.num_programs(1)

    @pl.when(j == 0)
    def _():
        xe = jnp.concatenate([xh_ref[...], x_ref[...]], axis=0)
        h_scr[...] = _rms(xe, gpre_ref[...]).astype(BF16)

    h = h_scr[...]
    gate = jnp.dot(h, wg_ref[...], preferred_element_type=F32)
    val = jnp.dot(h[FFN_HALO:], wv_ref[...], preferred_element_type=F32)
    row = lax.broadcasted_iota(jnp.int32, gate.shape, 0)
    seq_start = (i % tiles_per_seq) == 0
    gate = jnp.where(jnp.logical_and(row < FFN_HALO, seq_start), 0.0, gate)
    cw = cw_ref[...]
    conv = cb_ref[...] + gate[FFN_HALO:] * cw[2:3]
    conv = conv + gate[FFN_HALO - 2:-2] * cw[0:1]
    conv = conv + gate[FFN_HALO - 1:-1] * cw[1:2]
    act = _gelu_tanh(conv) * val
    part = jnp.dot(act.astype(BF16), wd_ref[...], preferred_element_type=F32)

    @pl.when(j == 0)
    def _():
        acc_scr[...] = part

    @pl.when(j > 0)
    def _():
        acc_scr[...] += part

    @pl.when(j == nj - 1)
    def _():
        y_ref[...] = x_ref[...] + _rms(acc_scr[...], gpost_ref[...])


def _conv_ffn(x2d, seq_len, g_pre, w_up, conv_w, conv_b, w_down, g_post, tm=512, nf=2):
    n, d = x2d.shape
    tf = D_FF // nf
    hb = tm // FFN_HALO
    return pl.pallas_call(
        functools.partial(_ffn_kernel, tm=tm, tiles_per_seq=seq_len // tm),
        grid=(n // tm, nf),
        in_specs=[pl.BlockSpec((FFN_HALO, d), lambda i, j: (jnp.maximum(i * hb - 1, 0), 0)),
                  pl.BlockSpec((tm, d), lambda i, j: (i, 0)),
                  pl.BlockSpec((1, d), lambda i, j: (0, 0)),
                  pl.BlockSpec((d, tf), lambda i, j: (0, j)),
                  pl.BlockSpec((d, tf), lambda i, j: (0, nf + j)),
                  pl.BlockSpec((CONV_WIDTH, tf), lambda i, j: (0, j)),
                  pl.BlockSpec((1, tf), lambda i, j: (0, j)),
                  pl.BlockSpec((tf, d), lambda i, j: (j, 0)),
                  pl.BlockSpec((1, d), lambda i, j: (0, 0))],
        out_specs=pl.BlockSpec((tm, d), lambda i, j: (i, 0)),
        out_shape=jax.ShapeDtypeStruct((n, d), F32),
        scratch_shapes=[pltpu.VMEM((tm + FFN_HALO, d), BF16), pltpu.VMEM((tm, d), F32)],
        compiler_params=_params("parallel", "arbitrary"),
        name="conv_ffn",
    )(x2d, x2d, g_pre.reshape(1, d), w_up, w_up, conv_w, conv_b.reshape(1, D_FF), w_down,
      g_post.reshape(1, d))


def _chunk_phase(start, c):
    return ((start // ATT_CHUNK + c) % (POS_PERIOD // ATT_CHUNK)) * ATT_CHUNK


def _softmax_cols_chunks(parts, offs, extra=None):
    m = extra
    for part, off in zip(parts, offs):
        cm = jnp.max(part, axis=0, keepdims=True) + off
        m = cm if m is None else jnp.maximum(m, cm)
    m = jnp.where(m == NEG_INF, 0.0, m)
    p = jnp.concatenate([jnp.exp2(part - (m - off)) for part, off in zip(parts, offs)], axis=0)
    return p, m


def _swa_kernel(slope_ref, sink_ref, q_ref, s3_ref, k_ref, v_ref, o_ref, *, n_rep, window):
    g = pl.program_id(1)
    i = pl.program_id(2)
    t0 = i * TQ
    d = HEAD_DIM
    ck = ATT_CHUNK
    n_c = (window + TQ) // ck
    start = pl.multiple_of(jnp.maximum(t0 - window, 0), LANES)
    kb = k_ref[0, 0, pl.ds(start, n_c * ck), :]
    vb = v_ref[0, 0, :, pl.ds(start, n_c * ck)]
    t_row = t0 + lax.broadcasted_iota(jnp.int32, (1, TQ), 1)
    key_in_chunk = lax.broadcasted_iota(jnp.int32, (ck, TQ), 0)
    valid = []
    for c in range(n_c):
        dist = t_row - (start + c * ck + key_in_chunk)
        valid.append(jnp.logical_and(dist >= 0, dist < window))
    rel_t = (t_row - start).astype(F32)
    scores = []
    for r in range(n_rep):
        q_aug = jnp.concatenate([q_ref[0, 0, r], s3_ref[0, r]], axis=0)
        scores.append(jnp.dot(kb, q_aug, preferred_element_type=F32))
    probs, tails = [], []
    for r in range(n_rep):
        slope = slope_ref[g * n_rep + r]
        parts = [jnp.where(valid[c], scores[r][c * ck:(c + 1) * ck], NEG_INF) for c in range(n_c)]
        offs = [slope * (c * ck - _chunk_phase(start, c)).astype(F32) for c in range(n_c)]
        sink = sink_ref[g * n_rep + r] * LOG2E + slope * rel_t
        p, m = _softmax_cols_chunks(parts, offs, extra=sink)
        probs.append(p.astype(BF16))
        tails.append(jnp.exp2(sink - m))
    outs = []
    for r in range(n_rep):
        pv = jnp.dot(vb, probs[r], preferred_element_type=F32)
        outs.append(pv[0:d] / (pv[d:d + 1] + tails[r]))
    o_ref[0] = jnp.concatenate(outs, axis=0).astype(o_ref.dtype)


def _swa_attention(slopes, sinks, q_t, slope_rows, k, v_t):
    b, g, r, d, t = q_t.shape
    ka, va = k.shape[3], v_t.shape[2]
    smem = pl.BlockSpec(memory_space=pltpu.SMEM)
    return pl.pallas_call(
        functools.partial(_swa_kernel, n_rep=r, window=SWA_WINDOW),
        grid=(b, g, t // TQ),
        in_specs=[smem, smem,
                  pl.BlockSpec((1, 1, r, d, TQ), lambda bi, gi, i: (bi, gi, 0, 0, i)),
                  pl.BlockSpec((1, r, ka - d, TQ), lambda bi, gi, i: (gi, 0, 0, 0)),
                  pl.BlockSpec((1, 1, t, ka), lambda bi, gi, i: (bi, gi, 0, 0)),
                  pl.BlockSpec((1, 1, va, t), lambda bi, gi, i: (bi, gi, 0, 0))],
        out_specs=pl.BlockSpec((1, r * d, TQ), lambda bi, gi, i: (bi, gi, i)),
        out_shape=jax.ShapeDtypeStruct((b, g * r * d, t), BF16),
        compiler_params=_params("parallel", "parallel", "parallel"),
        name="swa_attention",
    )(slopes, sinks, q_t, slope_rows, k, v_t)


def _rwkv_kernel(zr_ref, zk_ref, zv_ref, zl_ref, mur_ref, muk_ref, muv_ref, mul_ref,
                 w0_ref, a0_ref, kk_ref, ka_ref, rk_ref, lng_ref, lnb_ref,
                 w2_ref, a2_ref, g2_ref, o_ref, prev_scr, s_scr, *, nb):
    c = pl.program_id(2)
    cs = RWKV_CHUNK
    wd = RWKV_QUAD * HEAD_DIM
    each = lambda f, *cols: [f(*args) for args in zip(*cols)]

    @pl.when(c == 0)
    def _():
        prev_scr[...] = jnp.zeros_like(prev_scr)
        s_scr[...] = jnp.zeros_like(s_scr)

    row = lax.broadcasted_iota(jnp.int32, (cs, wd), 0)

    def shifted(z_ref, mu_ref, idx):
        outs = []
        for bb in range(nb):
            z = z_ref[bb]
            prev = prev_scr[idx * nb + bb, 0:1, :]
            zp = jnp.where(row == 0, prev, pltpu.roll(z, 1, 0))
            prev_scr[idx * nb + bb, 0:1, :] = z[cs - 1:cs, :]
            outs.append(z + (zp - z) * mu_ref[...])
        return outs

    r = shifted(zr_ref, mur_ref, 0)
    k = shifted(zk_ref, muk_ref, 1)
    v = shifted(zv_ref, muv_ref, 2)
    lo = shifted(zl_ref, mul_ref, 3)

    yw = each(lambda x: w0_ref[...] + _mm(jnp.tanh(x), w2_ref[...]), lo)
    ya = each(lambda x: a0_ref[...] + _mm(x, a2_ref[...]), lo)
    gate = each(lambda x: _mm(_sigmoid(x), g2_ref[...]), lo)
    softplus = each(lambda x: jnp.maximum(-x, 0.0) + jnp.log(1.0 + jnp.exp(-jnp.abs(x))), yw)
    logd = each(lambda x: -jnp.exp(-x - 0.5), softplus)
    a = each(_sigmoid, ya)

    i0 = lax.broadcasted_iota(jnp.int32, (wd, wd), 0)
    i1 = lax.broadcasted_iota(jnp.int32, (wd, wd), 1)
    same_head = (i0 // HEAD_DIM) == (i1 // HEAD_DIM)
    ones_head = jnp.where(same_head, 1.0, 0.0).astype(BF16)

    def head_sum(x):
        return _mm_exact_rhs(x, ones_head, 2)

    kk = each(lambda x: x * kk_ref[...], k)
    kk_ss = each(lambda x: head_sum(x * x), kk)
    kk = each(lambda x, ss: x / jnp.maximum(jnp.sqrt(ss), 1e-12), kk, kk_ss)
    kmod = each(lambda x, y: x * (1.0 + (y - 1.0) * ka_ref[...]), k, a)
    b = each(lambda x, y: x * y, kk, a)

    c0 = lax.broadcasted_iota(jnp.int32, (cs, cs), 0)
    c1 = lax.broadcasted_iota(jnp.int32, (cs, cs), 1)
    tri = jnp.where(c0 >= c1, 1.0, 0.0).astype(BF16)
    cl = each(lambda x: _mm_exact_lhs(tri, x, 3), logd)
    cl_end = each(lambda x: x[cs - 1:cs, :], cl)
    w_inc = each(jnp.exp, cl)
    w_exc = each(lambda x, y: jnp.exp(x - y), cl, logd)
    w_inv = each(lambda x: jnp.exp(-x), cl)
    w_rem = each(lambda x, y: jnp.exp(y - x), cl, cl_end)

    def stack(x):
        return jnp.where(same_head, jnp.concatenate([x] * RWKV_QUAD, axis=0), 0.0)

    mul_stack = lambda x, y: stack(x * y)
    x_kk = each(mul_stack, kk, w_exc)
    x_r = each(mul_stack, r, w_inc)
    x_b = each(mul_stack, b, w_inv)
    x_k = each(mul_stack, kmod, w_inv)
    x_v = each(stack, v)
    x_bd = each(mul_stack, b, w_rem)
    x_kd = each(mul_stack, kmod, w_rem)

    strict = (i0 % cs) > (i1 % cs)
    incl = (i0 % cs) >= (i1 % cs)
    a_ub = each(lambda x, y: jnp.where(strict, _mm_nt(x, y), 0.0), x_kk, x_b)
    a_uk = each(lambda x, y: jnp.where(strict, _mm_nt(x, y), 0.0), x_kk, x_k)
    a_rb = each(lambda x, y: jnp.where(incl, _mm_nt(x, y), 0.0), x_r, x_b)
    a_rk = each(lambda x, y: jnp.where(incl, _mm_nt(x, y), 0.0), x_r, x_k)

    eye = i0 == i1
    pw = a_ub
    t_inv = each(lambda x: jnp.where(eye, 1.0, 0.0) - x, a_ub)
    for _ in range(int(math.log2(cs)) - 1):
        pw = each(lambda x: _mm(x, x), pw)
        t_inv = each(lambda t, x: t + _mm(t, x), t_inv, pw)

    p1 = each(_mm, t_inv, x_kk)
    av = each(_mm, a_uk, x_v)
    p2 = each(_mm, t_inv, av)
    q1 = each(lambda x, y, z: x - _mm(y, z), x_r, a_rb, p1)
    q2 = each(lambda w, x, y, z: _mm(w, x) - _mm(y, z), a_rk, x_v, a_rb, p2)
    g_mat = each(lambda e, x, y: jnp.where(eye, jnp.exp(e), 0.0) - _mm_tn(x, y), cl_end, x_bd, p1)
    h_mat = each(lambda w, x, y, z: _mm_tn(w, x) - _mm_tn(y, z), x_kd, x_v, x_bd, p2)

    s0 = [s_scr[bb] for bb in range(nb)]
    ys = each(lambda x, s, y: _mm(x, s) + y, q1, s0, q2)
    s1 = each(lambda x, s, y: _mm(x, s) + y, g_mat, s0, h_mat)
    for bb in range(nb):
        s_scr[bb] = s1[bb]

    def fold(x):
        y = x[0:cs]
        for hd in range(1, RWKV_QUAD):
            y = y + x[hd * cs:(hd + 1) * cs]
        return y

    y = each(fold, ys)

    inv_n = 1.0 / HEAD_DIM
    mean = each(lambda x: head_sum(x) * inv_n, y)
    yc = each(lambda x, m: x - m, y, mean)
    var = each(lambda x: head_sum(x * x) * inv_n, yc)
    yn = each(lambda x, s: x * lax.rsqrt(s + RWKV_GN_EPS) * lng_ref[...] + lnb_ref[...], yc, var)
    bonus = each(lambda x, y, z: head_sum(x * y * rk_ref[...]) * z, r, kmod, v)
    for bb in range(nb):
        o_ref[bb] = (yn[bb] + bonus[bb]) * gate[bb]


def _rwkv_time_mix(z3, mu, w0, w2, a0, a2, g2, k_k, k_a, r_k, ln_g, ln_b):
    b, t, _ = z3.shape
    wd = RWKV_QUAD * HEAD_DIM
    nq = RWKV_DIM // wd
    base = SWA_COLS // wd
    lora_rows = RWKV_W_LORA + RWKV_A_LORA + RWKV_G_LORA
    assert lora_rows == wd and SWA_COLS % wd == 0

    def pad_rows(w, start):
        return jnp.zeros((wd, RWKV_DIM), F32).at[start:start + w.shape[0]].set(w).astype(BF16)

    w2p = pad_rows(w2, 0)
    a2p = pad_rows(a2, RWKV_W_LORA)
    g2p = pad_rows(g2, RWKV_W_LORA + RWKV_A_LORA)
    mu2 = mu.reshape(1, -1)
    row = lambda x: x.reshape(1, RWKV_DIM)
    nb = RWKV_BATCH if b % RWKV_BATCH == 0 else 1
    zspec = lambda off: pl.BlockSpec((nb, RWKV_CHUNK, wd), lambda bi, qi, ci: (bi, ci, base + off + qi))
    zlspec = pl.BlockSpec((nb, RWKV_CHUNK, wd), lambda bi, qi, ci: (bi, ci, base + 3 * nq))
    muspec = lambda off: pl.BlockSpec((1, wd), lambda bi, qi, ci: (0, off + qi))
    mulspec = pl.BlockSpec((1, wd), lambda bi, qi, ci: (0, 3 * nq))
    pspec = pl.BlockSpec((1, wd), lambda bi, qi, ci: (0, qi))
    wspec = pl.BlockSpec((wd, wd), lambda bi, qi, ci: (0, qi))
    return pl.pallas_call(
        functools.partial(_rwkv_kernel, nb=nb),
        grid=(b // nb, nq, t // RWKV_CHUNK),
        in_specs=[zspec(0), zspec(nq), zspec(2 * nq), zlspec,
                  muspec(0), muspec(nq), muspec(2 * nq), mulspec,
                  pspec, pspec, pspec, pspec, pspec, pspec, pspec,
                  wspec, wspec, wspec],
        out_specs=pl.BlockSpec((nb, RWKV_CHUNK, wd), lambda bi, qi, ci: (bi, ci, qi)),
        out_shape=jax.ShapeDtypeStruct((b, t, RWKV_DIM), F32),
        scratch_shapes=[pltpu.VMEM((4 * nb, 8, wd), F32), pltpu.VMEM((nb, wd, wd), F32)],
        compiler_params=_params("parallel", "parallel", "arbitrary"),
        name="rwkv7_time_mix",
    )(z3, z3, z3, z3, mu2, mu2, mu2, mu2,
      row(w0), row(a0), row(k_k), row(k_a), row(r_k), row(ln_g), row(ln_b),
      w2p, a2p, g2p)


def _compress_kernel(kb_ref, vb_ref, posk_ref, posv_ref, w1k_ref, w1v_ref, w2k_ref, w2v_ref,
                     kc_ref, vc_ref, *, n_cmp):
    half = CMP_STRIDE * HEAD_DIM

    def run(blk_ref, pos_ref, w1_ref, w2_ref):
        a = blk_ref[0, 0]
        lo = _mm(a + pos_ref[0:1, :], w1_ref[0:half, :])
        hi = _mm(a + pos_ref[1:2, :], w1_ref[half:2 * half, :])
        n_half = a.shape[0]
        hid = lo + pltpu.roll(hi, n_half - 1, 0)
        out = _mm(_gelu_tanh(hid), w2_ref[...])
        rows = lax.broadcasted_iota(jnp.int32, out.shape, 0)
        return jnp.where(rows < n_cmp, out, 0.0)

    kc_ref[0, 0] = run(kb_ref, posk_ref, w1k_ref, w2k_ref).astype(BF16)
    vc_ref[0, 0] = run(vb_ref, posv_ref, w1v_ref, w2v_ref).T.astype(BF16)


def _compress(kc_blocks, vc_blocks, pos_k, pos_v, w1_k, w1_v, w2_k, w2_v, n_cmp):
    b, g, nh, hw = kc_blocks.shape
    blk = pl.BlockSpec((1, 1, nh, hw), lambda bi, gi: (bi, gi, 0, 0))
    full = lambda a: pl.BlockSpec(a.shape, lambda bi, gi: (0,) * a.ndim)
    return pl.pallas_call(
        functools.partial(_compress_kernel, n_cmp=n_cmp),
        grid=(b, g),
        in_specs=[blk, blk, full(pos_k), full(pos_v), full(w1_k), full(w1_v), full(w2_k), full(w2_v)],
        out_specs=[pl.BlockSpec((1, 1, nh, HEAD_DIM), lambda bi, gi: (bi, gi, 0, 0)),
                   pl.BlockSpec((1, 1, HEAD_DIM, nh), lambda bi, gi: (bi, gi, 0, 0))],
        out_shape=[jax.ShapeDtypeStruct((b, g, nh, HEAD_DIM), BF16),
                   jax.ShapeDtypeStruct((b, g, HEAD_DIM, nh), BF16)],
        compiler_params=_params("parallel", "parallel"),
        name="nsa_compress",
    )(kc_blocks, vc_blocks, pos_k, pos_v, w1_k, w1_v, w2_k, w2_v)


def _nsa_kernel(slope_ref, q_ref, s3_ref, kc_ref, vc_ref, ks_ref, vs_ref, kw_ref, vw_ref, gl_ref, ovw_ref,
                o_ref, sel_scr, list_scr, valid_scr, sa_scr, sb_scr, m_scr, acc_scr, *, n_rep, n_blk):
    g = pl.program_id(1)
    i = pl.program_id(2)
    t0 = i * TQ
    d = HEAD_DIM
    ck = ATT_CHUNK
    gsz = SEL_GROUP
    bpg = gsz // SEL_LEN
    n_grp = n_blk // bpg
    q_aug = [jnp.concatenate([q_ref[0, 0, r], s3_ref[0, r]], axis=0) for r in range(n_rep)]
    slopes = [slope_ref[g * n_rep + r] for r in range(n_rep)]
    t_row = t0 + lax.broadcasted_iota(jnp.int32, (1, TQ), 1)
    key_in_chunk = lax.broadcasted_iota(jnp.int32, (ck, TQ), 0)
    key_in_blk = lax.broadcasted_iota(jnp.int32, (SEL_LEN, TQ), 0)

    kc = kc_ref[0, 0]
    vc = vc_ref[0, 0]
    s_cmp = [jnp.dot(kc, q_aug[r], preferred_element_type=F32) for r in range(n_rep)]
    n_wc = (NSA_WINDOW + TQ) // ck
    start = pl.multiple_of(jnp.maximum(t0 - NSA_WINDOW, 0), LANES)
    kw_band = kw_ref[0, 0, pl.ds(start, n_wc * ck), :]
    vw_band = vw_ref[0, 0, :, pl.ds(start, n_wc * ck)]
    s_win = [jnp.dot(kw_band, q_aug[r], preferred_element_type=F32) for r in range(n_rep)]

    def score_group(gidx, s_scr):
        k_cat = ks_ref[0, 0, pl.ds(pl.multiple_of(gidx * gsz, gsz), gsz), :]
        for r in range(n_rep):
            s_scr[r] = jnp.dot(k_cat, q_aug[r], preferred_element_type=F32)

    diag_group = (t0 + TQ - 1) // gsz
    score_group(diag_group, sb_scr)

    cmp_end = lax.broadcasted_iota(jnp.int32, (kc.shape[0], TQ), 0) * CMP_STRIDE + (CMP_LEN - 1)
    valid_c = cmp_end <= t_row
    p_sum = None
    o_cmp = []
    for r in range(n_rep):
        s = jnp.where(valid_c, s_cmp[r], NEG_INF)
        m = jnp.max(s, axis=0, keepdims=True)
        m = jnp.where(m == NEG_INF, 0.0, m)
        p = jnp.exp2(s - m)
        den = jnp.sum(p, axis=0, keepdims=True)
        p = p * (1.0 / jnp.where(den > 0, den, 1.0))
        p_sum = p if p_sum is None else p_sum + p
        o_cmp.append(jnp.dot(vc, p.astype(BF16), preferred_element_type=F32))

    o_win = []
    win_valid = []
    for c in range(n_wc):
        dist = t_row - (start + c * ck + key_in_chunk)
        win_valid.append(jnp.logical_and(dist >= 0, dist < NSA_WINDOW))
    for r in range(n_rep):
        parts = [jnp.where(win_valid[c], s_win[r][c * ck:(c + 1) * ck], NEG_INF) for c in range(n_wc)]
        offs = [slopes[r] * (c * ck - _chunk_phase(start, c)).astype(F32) for c in range(n_wc)]
        p, _ = _softmax_cols_chunks(parts, offs)
        pv = jnp.dot(vw_band, p.astype(BF16), preferred_element_type=F32)
        den = pv[d:d + 1]
        o_win.append(pv[0:d] / jnp.where(den > 0, den, 1.0))

    imp = _mm_exact_lhs(ovw_ref[...], p_sum, 3)
    blk = lax.broadcasted_iota(jnp.int32, (n_blk, TQ), 0)
    cur = (t0 + lax.broadcasted_iota(jnp.int32, (n_blk, TQ), 1)) // SEL_LEN
    forced = jnp.logical_or(blk == 0, jnp.logical_or(blk == cur, blk == cur - 1))
    score = jnp.where(forced, jnp.inf, jnp.where(blk > cur, NEG_INF, imp))
    sel = jnp.zeros((n_blk, TQ), F32)
    for _ in range(min(N_SEL, n_blk)):
        best = jnp.max(score, axis=0, keepdims=True)
        first = jnp.min(jnp.where(score == best, blk, n_blk), axis=0, keepdims=True)
        hit = blk == first
        sel = jnp.where(hit, 1.0, sel)
        score = jnp.where(hit, NEG_INF, score)
    sel_scr[...] = sel

    ones = jnp.ones((8, TQ), BF16)
    cnt = _mm_nt(ones, sel)
    pool = (lax.broadcasted_iota(jnp.int32, (n_blk, LANES), 0) // bpg
            == lax.broadcasted_iota(jnp.int32, (n_blk, LANES), 1))
    cnt_g = _mm(cnt, jnp.where(pool, 1.0, 0.0))
    for j in range(n_grp + 1):
        list_scr[j] = 0
        valid_scr[j] = 0
    pos = jnp.int32(n_grp)
    for j in reversed(range(n_grp)):
        hit = jnp.logical_and(cnt_g[0, j] > 0.0, j < diag_group)
        pos = pos - hit.astype(jnp.int32)
        slot = jnp.where(hit, pos, n_grp)
        list_scr[slot] = j
        valid_scr[slot] = 1
    n_pairs = (n_grp - pos + 1) // 2
    first = n_grp - 2 * n_pairs

    m_scr[...] = jnp.full(m_scr.shape, NEG_INF, F32)
    acc_scr[...] = jnp.zeros(acc_scr.shape, F32)

    def reduce_group(gidx, live, s_scr, diagonal):
        k0 = pl.multiple_of(gidx * gsz, gsz)
        rows = sel_scr[pl.ds(pl.multiple_of(gidx * bpg, bpg), bpg), :] * live
        v_cat = vs_ref[0, 0, :, pl.ds(k0, gsz)]
        masks = []
        for u in range(bpg):
            picked = rows[u:u + 1, :] > 0.5
            if diagonal:
                picked = jnp.logical_and(picked, k0 + u * SEL_LEN + key_in_blk <= t_row)
            masks.append(picked)
        rel = (k0 - t0).astype(F32)
        for r in range(n_rep):
            parts = [jnp.where(masks[u], s_scr[r, pl.ds(u * SEL_LEN, SEL_LEN), :], NEG_INF) for u in range(bpg)]
            off = slopes[r] * rel
            top = parts[0]
            for part in parts[1:]:
                top = jnp.maximum(top, part)
            m_old = m_scr[r]
            m_new = jnp.maximum(m_old, jnp.max(top, axis=0, keepdims=True) + off)
            m_safe = jnp.where(m_new == NEG_INF, 0.0, m_new)
            alpha = jnp.exp2(m_old - m_safe)
            sub = m_safe - off
            p = jnp.concatenate([jnp.exp2(part - sub) for part in parts], axis=0)
            acc_scr[r] = alpha * acc_scr[r] + jnp.dot(v_cat, p.astype(BF16), preferred_element_type=F32)
            m_scr[r] = m_new

    score_group(list_scr[first], sa_scr)
    reduce_group(diag_group, 1.0, sb_scr, True)

    def body(pi, carry):
        a = first + 2 * pi
        score_group(list_scr[a + 1], sb_scr)
        reduce_group(list_scr[a], valid_scr[a].astype(F32), sa_scr, False)
        score_group(list_scr[a + 2], sa_scr)
        reduce_group(list_scr[a + 1], valid_scr[a + 1].astype(F32), sb_scr, False)
        return carry

    lax.fori_loop(0, n_pairs, body, 0)

    gates = _sigmoid(gl_ref[0, 0])
    outs = []
    for r in range(n_rep):
        acc = acc_scr[r]
        l_fin = acc[d:d + 1]
        o_slc = acc[0:d] / jnp.where(l_fin > 0, l_fin, 1.0)
        g0 = gates[N_BRANCH * r + 0:N_BRANCH * r + 1, :]
        g1 = gates[N_BRANCH * r + 1:N_BRANCH * r + 2, :]
        g2 = gates[N_BRANCH * r + 2:N_BRANCH * r + 3, :]
        outs.append(g0 * o_cmp[r] + g1 * o_slc + g2 * o_win[r])
    o_ref[0] = jnp.concatenate(outs, axis=0).astype(o_ref.dtype)


def _nsa_attention(slopes, q_t, slope_rows, kc, vc_t, ks, vs_t, kw, vw_t, gl_t, ov_w):
    b, g, r, d, t = q_t.shape
    n_blk = t // SEL_LEN
    n_grp = t // SEL_GROUP
    ncp = kc.shape[2]
    ka = kc.shape[3]
    va = vs_t.shape[2]
    idx4 = lambda bi, gi, i: (bi, gi, 0, 0)
    return pl.pallas_call(
        functools.partial(_nsa_kernel, n_rep=r, n_blk=n_blk),
        grid=(b, g, t // TQ),
        in_specs=[pl.BlockSpec(memory_space=pltpu.SMEM),
                  pl.BlockSpec((1, 1, r, d, TQ), lambda bi, gi, i: (bi, gi, 0, 0, i)),
                  pl.BlockSpec((1, r, ka - d, TQ), lambda bi, gi, i: (gi, 0, 0, 0)),
                  pl.BlockSpec((1, 1, ncp, ka), idx4),
                  pl.BlockSpec((1, 1, d, ncp), idx4),
                  pl.BlockSpec((1, 1, t, ka), idx4),
                  pl.BlockSpec((1, 1, va, t), idx4),
                  pl.BlockSpec((1, 1, t, ka), idx4),
                  pl.BlockSpec((1, 1, va, t), idx4),
                  pl.BlockSpec((1, 1, r * N_BRANCH, TQ), lambda bi, gi, i: (bi, gi, 0, i)),
                  pl.BlockSpec(ov_w.shape, lambda bi, gi, i: (0, 0))],
        out_specs=pl.BlockSpec((1, r * d, TQ), lambda bi, gi, i: (bi, gi, i)),
        out_shape=jax.ShapeDtypeStruct((b, g * r * d, t), BF16),
        scratch_shapes=[pltpu.VMEM((n_blk, TQ), F32),
                        pltpu.SMEM((n_grp + 2,), jnp.int32),
                        pltpu.SMEM((n_grp + 2,), jnp.int32),
                        pltpu.VMEM((r, SEL_GROUP, TQ), F32),
                        pltpu.VMEM((r, SEL_GROUP, TQ), F32),
                        pltpu.VMEM((r, 1, TQ), F32),
                        pltpu.VMEM((r, va, TQ), F32)],
        compiler_params=_params("parallel", "parallel", "arbitrary"),
        name="nsa_attention",
    )(slopes, q_t, slope_rows, kc, vc_t, ks, vs_t, kw, vw_t, gl_t, ov_w)


def _alibi_slopes(n_heads):
    return jnp.exp2(-8.0 * jnp.arange(1, n_heads + 1, dtype=F32) / n_heads) * LOG2E


def _truncation_split_bf16(x, n):
    parts = []
    for _ in range(n):
        top = lax.bitcast_convert_type(lax.bitcast_convert_type(x, jnp.uint32) & jnp.uint32(0xFFFF0000), F32)
        parts.append(top.astype(BF16))
        x = x - top
    return parts


def _slope_query_rows(n_heads, n_groups):
    parts = _truncation_split_bf16(_alibi_slopes(n_heads), 3)
    rows = jnp.zeros((n_heads, KEY_AUG), BF16)
    for j, p in enumerate(parts):
        rows = rows.at[:, j].set(p).at[:, 3 + j].set(p)
    rows = jnp.broadcast_to(rows[:, :, None], (n_heads, KEY_AUG, TQ))
    return rows.reshape(n_groups, n_heads // n_groups, KEY_AUG, TQ)


def _augment_keys(k_tm, lo, hi):
    b, g, n, _ = k_tm.shape
    extra = jnp.zeros((n, KEY_AUG), F32)
    extra = extra.at[:, 0:3].set(lo.astype(F32)[:, None]).at[:, 3:6].set(hi.astype(F32)[:, None]).astype(BF16)
    return jnp.concatenate([k_tm, jnp.broadcast_to(extra, (b, g, n, KEY_AUG))], axis=-1)


def _token_key_positions(t):
    pos = jnp.arange(t) % POS_PERIOD
    lo = pos % ATT_CHUNK
    return lo, pos - lo


def _augment_values(v_fm):
    b, g, _, n = v_fm.shape
    extra = jnp.zeros((VAL_AUG, n), BF16).at[0].set(1.0)
    return jnp.concatenate([v_fm, jnp.broadcast_to(extra, (b, g, VAL_AUG, n))], axis=2)


def _heads_feature_major(a, b, t, g):
    return a.reshape(b, t, g, HEAD_DIM).transpose(0, 2, 3, 1)


def _heads_token_major(a, b, t, g):
    return a.reshape(b, t, g, HEAD_DIM).transpose(0, 2, 1, 3)


def _hybrid_layer(x3d, g_pre, g_post, w_in, w_out, sinks, mu, w0, w2, a0, a2, g2, k_k, k_a,
                  r_k, ln_g, ln_b):
    b, t, d = x3d.shape
    z = _norm_proj(x3d.reshape(b * t, d), g_pre, w_in.astype(BF16))
    gq, rq = SWA_KV_HEADS, SWA_HEADS // SWA_KV_HEADS
    scale = HEAD_DIM ** -0.5 * LOG2E
    lo, hi = _token_key_positions(t)
    q_t = (z[:, :SWA_Q_COLS] * scale).astype(BF16).reshape(b, t, gq, rq, HEAD_DIM).transpose(0, 2, 3, 4, 1)
    k = _augment_keys(_heads_token_major(z[:, SWA_Q_COLS:SWA_Q_COLS + SWA_KV_COLS].astype(BF16), b, t, gq), lo, hi)
    v_t = _augment_values(_heads_feature_major(z[:, SWA_Q_COLS + SWA_KV_COLS:SWA_COLS].astype(BF16), b, t, gq))
    o_a = _swa_attention(_alibi_slopes(SWA_HEADS), sinks.astype(F32), q_t, _slope_query_rows(SWA_HEADS, gq), k, v_t)
    o_b = _rwkv_time_mix(z.reshape(b, t, -1), mu, w0, w2, a0, a2, g2, k_k, k_a, r_k.reshape(-1), ln_g, ln_b)
    w_out = w_out.astype(BF16)
    return _out_proj([o_a, o_b], [True, False], [w_out[:SWA_Q_COLS], w_out[SWA_Q_COLS:]], x3d, g_post)


def _overlap_weights(n_blk, n_cmp_pad, n_cmp):
    ratio, span = SEL_LEN // CMP_STRIDE, CMP_LEN // CMP_STRIDE
    w = np.zeros((n_blk, n_cmp_pad), np.float32)
    for j in range(n_blk):
        for a in range(ratio):
            for s in range(span):
                c = ratio * j + a - s
                if 0 <= c < n_cmp:
                    w[j, c] += 1.0
    return jnp.asarray(w, BF16)


def _nsa_layer(x3d, g_pre, g_post, w_in, w_out, pos_k, w1_k, w2_k, pos_v, w1_v, w2_v):
    b, t, d = x3d.shape
    g, r = NSA_KV_HEADS, NSA_HEADS // NSA_KV_HEADS
    n_cols = w_in.shape[1]
    pad = (-n_cols) % LANES
    w_in = jnp.pad(w_in.astype(BF16), ((0, 0), (0, pad)))
    z = _norm_proj(x3d.reshape(b * t, d), g_pre, w_in)
    off = NSA_Q_COLS
    cols = lambda j: z[:, off + j * NSA_KV_COLS:off + (j + 1) * NSA_KV_COLS]
    scale = HEAD_DIM ** -0.5 * LOG2E
    q_t = (z[:, :NSA_Q_COLS] * scale).astype(BF16).reshape(b, t, g, r, HEAD_DIM).transpose(0, 2, 3, 4, 1)
    n_half = t // CMP_STRIDE
    n_cmp = (t - CMP_LEN) // CMP_STRIDE + 1
    half_rows = lambda a: a.reshape(b, n_half, CMP_STRIDE, g, HEAD_DIM).transpose(0, 3, 1, 2, 4).reshape(
        b, g, n_half, CMP_STRIDE * HEAD_DIM)
    flat_pos = lambda p: p.reshape(CMP_LEN // CMP_STRIDE, CMP_STRIDE * HEAD_DIM)
    flat_w1 = lambda w: w.reshape(CMP_LEN * HEAD_DIM, CMP_HIDDEN).astype(BF16)
    kc, vc_t = _compress(half_rows(cols(0)), half_rows(cols(1)), flat_pos(pos_k), flat_pos(pos_v),
                         flat_w1(w1_k), flat_w1(w1_v), w2_k.astype(BF16), w2_v.astype(BF16), n_cmp)
    lo, hi = _token_key_positions(t)
    cmp_pos = jnp.arange(n_half) * CMP_STRIDE
    kc = _augment_keys(kc, cmp_pos, jnp.zeros_like(cmp_pos))
    ks = _augment_keys(_heads_token_major(cols(2).astype(BF16), b, t, g), lo, hi)
    vs_t = _augment_values(_heads_feature_major(cols(3).astype(BF16), b, t, g))
    kw = _augment_keys(_heads_token_major(cols(4).astype(BF16), b, t, g), lo, hi)
    vw_t = _augment_values(_heads_feature_major(cols(5).astype(BF16), b, t, g))
    gl0 = off + 6 * NSA_KV_COLS
    gl_t = z[:, gl0:gl0 + NSA_HEADS * N_BRANCH].reshape(b, t, g, r * N_BRANCH).transpose(0, 2, 3, 1)
    ov_w = _overlap_weights(t // SEL_LEN, n_half, n_cmp)
    o = _nsa_attention(_alibi_slopes(NSA_HEADS), q_t, _slope_query_rows(NSA_HEADS, g), kc, vc_t, ks, vs_t,
                       kw, vw_t, gl_t, ov_w)
    return _out_proj([o], [True], [w_out.astype(BF16)], x3d, g_post)


def kernel(x, mix_pre_g, mix_post_g, ffn_pre_g, ffn_post_g, hy_w_in, hy_w_out, swa_sinks, rwkv_mu, rwkv_w0, rwkv_w2, rwkv_a0, rwkv_a2, rwkv_g2, rwkv_k_k, rwkv_k_a, rwkv_r_k, rwkv_ln_g, rwkv_ln_b, nsa_w_in, nsa_w_out, nsa_cmp_pos_k, nsa_cmp_w1_k, nsa_cmp_w2_k, nsa_cmp_pos_v, nsa_cmp_w1_v, nsa_cmp_w2_v, ffn_w_up, ffn_conv_w, ffn_conv_b, ffn_w_down):
    b, t, d = x.shape
    depth = mix_pre_g.shape[0]
    for layer in range(depth):
        i = layer // 2
        if layer % 2 == 0:
            x = _hybrid_layer(x, mix_pre_g[layer], mix_post_g[layer], hy_w_in[i], hy_w_out[i],
                              swa_sinks[i], rwkv_mu[i], rwkv_w0[i], rwkv_w2[i], rwkv_a0[i], rwkv_a2[i],
                              rwkv_g2[i], rwkv_k_k[i], rwkv_k_a[i], rwkv_r_k[i], rwkv_ln_g[i], rwkv_ln_b[i])
        else:
            x = _nsa_layer(x, mix_pre_g[layer], mix_post_g[layer], nsa_w_in[i], nsa_w_out[i],
                           nsa_cmp_pos_k[i], nsa_cmp_w1_k[i], nsa_cmp_w2_k[i],
                           nsa_cmp_pos_v[i], nsa_cmp_w1_v[i], nsa_cmp_w2_v[i])
        x = _conv_ffn(x.reshape(b * t, d), t, ffn_pre_g[layer], ffn_w_up[layer].astype(BF16), ffn_conv_w[layer],
                      ffn_conv_b[layer], ffn_w_down[layer].astype(BF16), ffn_post_g[layer]).reshape(b, t, d)
    return x
```

```python
import functools
import math

import numpy as np
import jax
import jax.numpy as jnp
from jax import lax
from jax.experimental import pallas as pl
from jax.experimental.pallas import tpu as pltpu

F32 = jnp.float32
BF16 = jnp.bfloat16

D_MODEL = 1024
HEAD_DIM = 64
SWA_HEADS = 8
SWA_KV_HEADS = 2
SWA_WINDOW = 128
RWKV_HEADS = 8
RWKV_DIM = RWKV_HEADS * HEAD_DIM
RWKV_W_LORA = 64
RWKV_A_LORA = 64
RWKV_G_LORA = 128
RWKV_GN_EPS = 64e-5
NSA_HEADS = 16
NSA_KV_HEADS = 4
CMP_LEN = 32
CMP_STRIDE = 16
CMP_HIDDEN = 256
SEL_LEN = 64
N_SEL = 8
NSA_WINDOW = 256
N_BRANCH = 3
D_FF = 2816
CONV_WIDTH = 3
NORM_EPS = 1e-6

SWA_Q_COLS = SWA_HEADS * HEAD_DIM
SWA_KV_COLS = SWA_KV_HEADS * HEAD_DIM
SWA_COLS = SWA_Q_COLS + 2 * SWA_KV_COLS
NSA_KV_COLS = NSA_KV_HEADS * HEAD_DIM
NSA_Q_COLS = NSA_HEADS * HEAD_DIM

LANES = 128
VMEM_LIMIT = 56 * 1024 * 1024

TQ = 256
ATT_CHUNK = 128
POS_PERIOD = 512
KEY_AUG = 16
VAL_AUG = 16
SEL_GROUP = POS_PERIOD
RWKV_CHUNK = 64
RWKV_QUAD = 4
RWKV_BATCH = 4
NEG_INF = float("-inf")
LOG2E = math.log2(math.e)


def _params(*sem):
    return pltpu.CompilerParams(dimension_semantics=sem, vmem_limit_bytes=VMEM_LIMIT)


def _rms(x, g):
    return x * lax.rsqrt(jnp.mean(x * x, axis=-1, keepdims=True) + NORM_EPS) * g


def _mm(a, b):
    return jnp.dot(a.astype(BF16), b.astype(BF16), preferred_element_type=F32)


def _mm_nt(a, b):
    return lax.dot_general(a.astype(BF16), b.astype(BF16), (((1,), (1,)), ((), ())),
                           preferred_element_type=F32)


def _mm_tn(a, b):
    return lax.dot_general(a.astype(BF16), b.astype(BF16), (((0,), (0,)), ((), ())),
                           preferred_element_type=F32)


def _split_bf16(x, n):
    parts = []
    for _ in range(n):
        p = x.astype(BF16)
        parts.append(p)
        x = x - p.astype(F32)
    return parts


def _mm_exact_lhs(a_bf16, x, n):
    out = None
    for p in _split_bf16(x, n):
        t = jnp.dot(a_bf16, p, preferred_element_type=F32)
        out = t if out is None else out + t
    return out


def _mm_exact_rhs(x, b_bf16, n):
    out = None
    for p in _split_bf16(x, n):
        t = jnp.dot(p, b_bf16, preferred_element_type=F32)
        out = t if out is None else out + t
    return out


def _sigmoid(x):
    return 1.0 / (1.0 + jnp.exp(-x))


def _gelu_tanh(x):
    c = math.sqrt(2.0 / math.pi)
    return 0.5 * x * (1.0 + jnp.tanh(c * (x + 0.044715 * (x * x * x))))


def _norm_proj_kernel(x_ref, g_ref, wq_ref, wb_ref, wf_ref, q_ref, ob_ref, of_ref):
    h = _rms(x_ref[0], g_ref[...]).astype(BF16)
    q_ref[0] = lax.dot_general(wq_ref[...], h, (((1,), (1,)), ((), ())),
                               preferred_element_type=F32).astype(q_ref.dtype)
    ob_ref[0] = jnp.dot(h, wb_ref[...], preferred_element_type=F32).astype(ob_ref.dtype)
    of_ref[0] = jnp.dot(h, wf_ref[...], preferred_element_type=F32)


def _norm_proj(x3d, g, wq_t, w_b, w_f, tm=512):
    b, t, d = x3d.shape
    fq, cb, cf = wq_t.shape[0], w_b.shape[1], w_f.shape[1]
    const = lambda bi, j: (0, 0)
    return pl.pallas_call(
        _norm_proj_kernel,
        grid=(b, t // tm),
        in_specs=[pl.BlockSpec((1, tm, d), lambda bi, j: (bi, j, 0)),
                  pl.BlockSpec((1, d), const),
                  pl.BlockSpec((fq, d), const),
                  pl.BlockSpec((d, cb), const),
                  pl.BlockSpec((d, cf), const)],
        out_specs=[pl.BlockSpec((1, fq, tm), lambda bi, j: (bi, 0, j)),
                   pl.BlockSpec((1, tm, cb), lambda bi, j: (bi, j, 0)),
                   pl.BlockSpec((1, tm, cf), lambda bi, j: (bi, j, 0))],
        out_shape=[jax.ShapeDtypeStruct((b, fq, t), BF16),
                   jax.ShapeDtypeStruct((b, t, cb), BF16),
                   jax.ShapeDtypeStruct((b, t, cf), F32)],
        compiler_params=_params("parallel", "parallel"),
        name="norm_proj",
    )(x3d, g.reshape(1, d), wq_t, w_b, w_f)


def _out_proj_kernel(*refs, feature_major):
    n_parts = len(feature_major)
    o_refs = refs[:n_parts]
    w_refs = refs[n_parts:2 * n_parts]
    x_ref, g_ref, y_ref = refs[2 * n_parts:]
    m = None
    for o_ref, w_ref, fm in zip(o_refs, w_refs, feature_major):
        if fm:
            t = lax.dot_general(o_ref[0].astype(BF16), w_ref[...], (((0,), (0,)), ((), ())),
                                preferred_element_type=F32)
        else:
            t = jnp.dot(o_ref[0].astype(BF16), w_ref[...], preferred_element_type=F32)
        m = t if m is None else m + t
    y_ref[0] = x_ref[0] + _rms(m, g_ref[...])


def _out_proj(parts, feature_major, weights, x3d, g, tm=512):
    b, t, d = x3d.shape
    in_specs = []
    for p, fm in zip(parts, feature_major):
        if fm:
            in_specs.append(pl.BlockSpec((1, p.shape[1], tm), lambda bi, j: (bi, 0, j)))
        else:
            in_specs.append(pl.BlockSpec((1, tm, p.shape[2]), lambda bi, j: (bi, j, 0)))
    in_specs += [pl.BlockSpec(w.shape, lambda bi, j: (0, 0)) for w in weights]
    in_specs += [pl.BlockSpec((1, tm, d), lambda bi, j: (bi, j, 0)), pl.BlockSpec((1, d), lambda bi, j: (0, 0))]
    return pl.pallas_call(
        functools.partial(_out_proj_kernel, feature_major=tuple(feature_major)),
        grid=(b, t // tm),
        in_specs=in_specs,
        out_specs=pl.BlockSpec((1, tm, d), lambda bi, j: (bi, j, 0)),
        out_shape=jax.ShapeDtypeStruct((b, t, d), F32),
        compiler_params=_params("parallel", "parallel"),
        name="out_proj",
    )(*parts, *weights, x3d, g.reshape(1, d))


FFN_HALO = 16


def _ffn_kernel(xh_ref, x_ref, gpre_ref, wg_ref, wv_ref, cw_ref, cb_ref, wd_ref, gpost_ref,
                y_ref, h_scr, acc_scr, *, tm, tiles_per_seq):
    i = pl.program_id(0)
    j = pl.program_id(1)
    nj = pl.num_programs(1)

    @pl.when(j == 0)
    def _():
        xe = jnp.concatenate([xh_ref[...], x_ref[...]], axis=0)
        h_scr[...] = _rms(xe, gpre_ref[...]).astype(BF16)

    h = h_scr[...]
    gate = jnp.dot(h, wg_ref[...], preferred_element_type=F32)
    val = jnp.dot(h[FFN_HALO:], wv_ref[...], preferred_element_type=F32)
    row = lax.broadcasted_iota(jnp.int32, gate.shape, 0)
    seq_start = (i % tiles_per_seq) == 0
    gate = jnp.where(jnp.logical_and(row < FFN_HALO, seq_start), 0.0, gate)
    cw = cw_ref[...]
    conv = cb_ref[...] + gate[FFN_HALO:] * cw[2:3]
    conv = conv + gate[FFN_HALO - 2:-2] * cw[0:1]
    conv = conv + gate[FFN_HALO - 1:-1] * cw[1:2]
    act = _gelu_tanh(conv) * val
    part = jnp.dot(act.astype(BF16), wd_ref[...], preferred_element_type=F32)

    @pl.when(j == 0)
    def _():
        acc_scr[...] = part

    @pl.when(j > 0)
    def _():
        acc_scr[...] += part

    @pl.when(j == nj - 1)
    def _():
        y_ref[...] = x_ref[...] + _rms(acc_scr[...], gpost_ref[...])


def _conv_ffn(x2d, seq_len, g_pre, w_up, conv_w, conv_b, w_down, g_post, tm=512, nf=2):
    n, d = x2d.shape
    tf = D_FF // nf
    hb = tm // FFN_HALO
    return pl.pallas_call(
        functools.partial(_ffn_kernel, tm=tm, tiles_per_seq=seq_len // tm),
        grid=(n // tm, nf),
        in_specs=[pl.BlockSpec((FFN_HALO, d), lambda i, j: (jnp.maximum(i * hb - 1, 0), 0)),
                  pl.BlockSpec((tm, d), lambda i, j: (i, 0)),
                  pl.BlockSpec((1, d), lambda i, j: (0, 0)),
                  pl.BlockSpec((d, tf), lambda i, j: (0, j)),
                  pl.BlockSpec((d, tf), lambda i, j: (0, nf + j)),
                  pl.BlockSpec((CONV_WIDTH, tf), lambda i, j: (0, j)),
                  pl.BlockSpec((1, tf), lambda i, j: (0, j)),
                  pl.BlockSpec((tf, d), lambda i, j: (j, 0)),
                  pl.BlockSpec((1, d), lambda i, j: (0, 0))],
        out_specs=pl.BlockSpec((tm, d), lambda i, j: (i, 0)),
        out_shape=jax.ShapeDtypeStruct((n, d), F32),
        scratch_shapes=[pltpu.VMEM((tm + FFN_HALO, d), BF16), pltpu.VMEM((tm, d), F32)],
        compiler_params=_params("parallel", "arbitrary"),
        name="conv_ffn",
    )(x2d, x2d, g_pre.reshape(1, d), w_up, w_up, conv_w, conv_b.reshape(1, D_FF), w_down,
      g_post.reshape(1, d))


def _chunk_phase(start, c):
    return ((start // ATT_CHUNK + c) % (POS_PERIOD // ATT_CHUNK)) * ATT_CHUNK


def _softmax_cols_chunks(parts, offs, extra=None):
    m = extra
    for part, off in zip(parts, offs):
        cm = jnp.max(part, axis=0, keepdims=True) + off
        m = cm if m is None else jnp.maximum(m, cm)
    m = jnp.where(m == NEG_INF, 0.0, m)
    p = jnp.concatenate([jnp.exp2(part - (m - off)) for part, off in zip(parts, offs)], axis=0)
    return p, m


def _swa_kernel(slope_ref, sink_ref, q_ref, s3_ref, k_ref, v_ref, o_ref, *, n_rep, window):
    g = pl.program_id(1)
    i = pl.program_id(2)
    t0 = i * TQ
    d = HEAD_DIM
    ck = ATT_CHUNK
    n_c = (window + TQ) // ck
    start = pl.multiple_of(jnp.maximum(t0 - window, 0), LANES)
    kb = k_ref[0, 0, pl.ds(start, n_c * ck), :]
    vb = v_ref[0, 0, :, pl.ds(start, n_c * ck)]
    t_row = t0 + lax.broadcasted_iota(jnp.int32, (1, TQ), 1)
    key_in_chunk = lax.broadcasted_iota(jnp.int32, (ck, TQ), 0)
    valid = []
    for c in range(n_c):
        dist = t_row - (start + c * ck + key_in_chunk)
        valid.append(jnp.logical_and(dist >= 0, dist < window))
    rel_t = (t_row - start).astype(F32)
    scores = []
    for r in range(n_rep):
        q_aug = jnp.concatenate([q_ref[0, 0, r], s3_ref[0, r]], axis=0)
        scores.append(jnp.dot(kb, q_aug, preferred_element_type=F32))
    probs, tails = [], []
    for r in range(n_rep):
        slope = slope_ref[g * n_rep + r]
        parts = [jnp.where(valid[c], scores[r][c * ck:(c + 1) * ck], NEG_INF) for c in range(n_c)]
        offs = [slope * (c * ck - _chunk_phase(start, c)).astype(F32) for c in range(n_c)]
        sink = sink_ref[g * n_rep + r] * LOG2E + slope * rel_t
        p, m = _softmax_cols_chunks(parts, offs, extra=sink)
        probs.append(p.astype(BF16))
        tails.append(jnp.exp2(sink - m))
    outs = []
    for r in range(n_rep):
        pv = jnp.dot(vb, probs[r], preferred_element_type=F32)
        outs.append(pv[0:d] / (pv[d:d + 1] + tails[r]))
    o_ref[0] = jnp.concatenate(outs, axis=0).astype(o_ref.dtype)


def _swa_attention(slopes, sinks, q_t, slope_rows, k, v_t):
    b, g, r, d, t = q_t.shape
    ka, va = k.shape[3], v_t.shape[2]
    smem = pl.BlockSpec(memory_space=pltpu.SMEM)
    return pl.pallas_call(
        functools.partial(_swa_kernel, n_rep=r, window=SWA_WINDOW),
        grid=(b, g, t // TQ),
        in_specs=[smem, smem,
                  pl.BlockSpec((1, 1, r, d, TQ), lambda bi, gi, i: (bi, gi, 0, 0, i)),
                  pl.BlockSpec((1, r, ka - d, TQ), lambda bi, gi, i: (gi, 0, 0, 0)),
                  pl.BlockSpec((1, 1, t, ka), lambda bi, gi, i: (bi, gi, 0, 0)),
                  pl.BlockSpec((1, 1, va, t), lambda bi, gi, i: (bi, gi, 0, 0))],
        out_specs=pl.BlockSpec((1, r * d, TQ), lambda bi, gi, i: (bi, gi, i)),
        out_shape=jax.ShapeDtypeStruct((b, g * r * d, t), BF16),
        compiler_params=_params("parallel", "parallel", "parallel"),
        name="swa_attention",
    )(slopes, sinks, q_t, slope_rows, k, v_t)


def _rwkv_kernel(zr_ref, zk_ref, zv_ref, zl_ref, mur_ref, muk_ref, muv_ref, mul_ref,
                 w0_ref, a0_ref, kk_ref, ka_ref, rk_ref, lng_ref, lnb_ref,
                 w2_ref, a2_ref, g2_ref, o_ref, prev_scr, s_scr, *, nb):
    c = pl.program_id(2)
    cs = RWKV_CHUNK
    wd = RWKV_QUAD * HEAD_DIM
    each = lambda f, *cols: [f(*args) for args in zip(*cols)]

    @pl.when(c == 0)
    def _():
        prev_scr[...] = jnp.zeros_like(prev_scr)
        s_scr[...] = jnp.zeros_like(s_scr)

    row = lax.broadcasted_iota(jnp.int32, (cs, wd), 0)

    def shifted(z_ref, mu_ref, idx):
        outs = []
        for bb in range(nb):
            z = z_ref[bb]
            prev = prev_scr[idx * nb + bb, 0:1, :]
            zp = jnp.where(row == 0, prev, pltpu.roll(z, 1, 0))
            prev_scr[idx * nb + bb, 0:1, :] = z[cs - 1:cs, :]
            outs.append(z + (zp - z) * mu_ref[...])
        return outs

    r = shifted(zr_ref, mur_ref, 0)
    k = shifted(zk_ref, muk_ref, 1)
    v = shifted(zv_ref, muv_ref, 2)
    lo = shifted(zl_ref, mul_ref, 3)

    yw = each(lambda x: w0_ref[...] + _mm(jnp.tanh(x), w2_ref[...]), lo)
    ya = each(lambda x: a0_ref[...] + _mm(x, a2_ref[...]), lo)
    gate = each(lambda x: _mm(_sigmoid(x), g2_ref[...]), lo)
    softplus = each(lambda x: jnp.maximum(-x, 0.0) + jnp.log(1.0 + jnp.exp(-jnp.abs(x))), yw)
    logd = each(lambda x: -jnp.exp(-x - 0.5), softplus)
    a = each(_sigmoid, ya)

    i0 = lax.broadcasted_iota(jnp.int32, (wd, wd), 0)
    i1 = lax.broadcasted_iota(jnp.int32, (wd, wd), 1)
    same_head = (i0 // HEAD_DIM) == (i1 // HEAD_DIM)
    ones_head = jnp.where(same_head, 1.0, 0.0).astype(BF16)

    def head_sum(x):
        return _mm_exact_rhs(x, ones_head, 2)

    kk = each(lambda x: x * kk_ref[...], k)
    kk_ss = each(lambda x: head_sum(x * x), kk)
    kk = each(lambda x, ss: x / jnp.maximum(jnp.sqrt(ss), 1e-12), kk, kk_ss)
    kmod = each(lambda x, y: x * (1.0 + (y - 1.0) * ka_ref[...]), k, a)
    b = each(lambda x, y: x * y, kk, a)

    c0 = lax.broadcasted_iota(jnp.int32, (cs, cs), 0)
    c1 = lax.broadcasted_iota(jnp.int32, (cs, cs), 1)
    tri = jnp.where(c0 >= c1, 1.0, 0.0).astype(BF16)
    cl = each(lambda x: _mm_exact_lhs(tri, x, 3), logd)
    cl_end = each(lambda x: x[cs - 1:cs, :], cl)
    w_inc = each(jnp.exp, cl)
    w_exc = each(lambda x, y: jnp.exp(x - y), cl, logd)
    w_inv = each(lambda x: jnp.exp(-x), cl)
    w_rem = each(lambda x, y: jnp.exp(y - x), cl, cl_end)

    def stack(x):
        return jnp.where(same_head, jnp.concatenate([x] * RWKV_QUAD, axis=0), 0.0)

    mul_stack = lambda x, y: stack(x * y)
    x_kk = each(mul_stack, kk, w_exc)
    x_r = each(mul_stack, r, w_inc)
    x_b = each(mul_stack, b, w_inv)
    x_k = each(mul_stack, kmod, w_inv)
    x_v = each(stack, v)
    x_bd = each(mul_stack, b, w_rem)
    x_kd = each(mul_stack, kmod, w_rem)

    strict = (i0 % cs) > (i1 % cs)
    incl = (i0 % cs) >= (i1 % cs)
    a_ub = each(lambda x, y: jnp.where(strict, _mm_nt(x, y), 0.0), x_kk, x_b)
    a_uk = each(lambda x, y: jnp.where(strict, _mm_nt(x, y), 0.0), x_kk, x_k)
    a_rb = each(lambda x, y: jnp.where(incl, _mm_nt(x, y), 0.0), x_r, x_b)
    a_rk = each(lambda x, y: jnp.where(incl, _mm_nt(x, y), 0.0), x_r, x_k)

    eye = i0 == i1
    pw = a_ub
    t_inv = each(lambda x: jnp.where(eye, 1.0, 0.0) - x, a_ub)
    for _ in range(int(math.log2(cs)) - 1):
        pw = each(lambda x: _mm(x, x), pw)
        t_inv = each(lambda t, x: t + _mm(t, x), t_inv, pw)

    p1 = each(_mm, t_inv, x_kk)
    av = each(_mm, a_uk, x_v)
    p2 = each(_mm, t_inv, av)
    q1 = each(lambda x, y, z: x - _mm(y, z), x_r, a_rb, p1)
    q2 = each(lambda w, x, y, z: _mm(w, x) - _mm(y, z), a_rk, x_v, a_rb, p2)
    g_mat = each(lambda e, x, y: jnp.where(eye, jnp.exp(e), 0.0) - _mm_tn(x, y), cl_end, x_bd, p1)
    h_mat = each(lambda w, x, y, z: _mm_tn(w, x) - _mm_tn(y, z), x_kd, x_v, x_bd, p2)

    s0 = [s_scr[bb] for bb in range(nb)]
    ys = each(lambda x, s, y: _mm(x, s) + y, q1, s0, q2)
    s1 = each(lambda x, s, y: _mm(x, s) + y, g_mat, s0, h_mat)
    for bb in range(nb):
        s_scr[bb] = s1[bb]

    def fold(x):
        y = x[0:cs]
        for hd in range(1, RWKV_QUAD):
            y = y + x[hd * cs:(hd + 1) * cs]
        return y

    y = each(fold, ys)

    inv_n = 1.0 / HEAD_DIM
    mean = each(lambda x: head_sum(x) * inv_n, y)
    yc = each(lambda x, m: x - m, y, mean)
    var = each(lambda x: head_sum(x * x) * inv_n, yc)
    yn = each(lambda x, s: x * lax.rsqrt(s + RWKV_GN_EPS) * lng_ref[...] + lnb_ref[...], yc, var)
    bonus = each(lambda x, y, z: head_sum(x * y * rk_ref[...]) * z, r, kmod, v)
    for bb in range(nb):
        o_ref[bb] = (yn[bb] + bonus[bb]) * gate[bb]


def _rwkv_time_mix(z3, mu, w0, w2, a0, a2, g2, k_k, k_a, r_k, ln_g, ln_b):
    b, t, _ = z3.shape
    wd = RWKV_QUAD * HEAD_DIM
    nq = RWKV_DIM // wd
    base = 0
    lora_rows = RWKV_W_LORA + RWKV_A_LORA + RWKV_G_LORA
    assert lora_rows == wd

    def pad_rows(w, start):
        return jnp.zeros((wd, RWKV_DIM), F32).at[start:start + w.shape[0]].set(w).astype(BF16)

    w2p = pad_rows(w2, 0)
    a2p = pad_rows(a2, RWKV_W_LORA)
    g2p = pad_rows(g2, RWKV_W_LORA + RWKV_A_LORA)
    mu2 = mu.reshape(1, -1)
    row = lambda x: x.reshape(1, RWKV_DIM)
    nb = RWKV_BATCH if b % RWKV_BATCH == 0 else 1
    zspec = lambda off: pl.BlockSpec((nb, RWKV_CHUNK, wd), lambda bi, qi, ci: (bi, ci, base + off + qi))
    zlspec = pl.BlockSpec((nb, RWKV_CHUNK, wd), lambda bi, qi, ci: (bi, ci, base + 3 * nq))
    muspec = lambda off: pl.BlockSpec((1, wd), lambda bi, qi, ci: (0, off + qi))
    mulspec = pl.BlockSpec((1, wd), lambda bi, qi, ci: (0, 3 * nq))
    pspec = pl.BlockSpec((1, wd), lambda bi, qi, ci: (0, qi))
    wspec = pl.BlockSpec((wd, wd), lambda bi, qi, ci: (0, qi))
    return pl.pallas_call(
        functools.partial(_rwkv_kernel, nb=nb),
        grid=(b // nb, nq, t // RWKV_CHUNK),
        in_specs=[zspec(0), zspec(nq), zspec(2 * nq), zlspec,
                  muspec(0), muspec(nq), muspec(2 * nq), mulspec,
                  pspec, pspec, pspec, pspec, pspec, pspec, pspec,
                  wspec, wspec, wspec],
        out_specs=pl.BlockSpec((nb, RWKV_CHUNK, wd), lambda bi, qi, ci: (bi, ci, qi)),
        out_shape=jax.ShapeDtypeStruct((b, t, RWKV_DIM), F32),
        scratch_shapes=[pltpu.VMEM((4 * nb, 8, wd), F32), pltpu.VMEM((nb, wd, wd), F32)],
        compiler_params=_params("parallel", "parallel", "arbitrary"),
        name="rwkv7_time_mix",
    )(z3, z3, z3, z3, mu2, mu2, mu2, mu2,
      row(w0), row(a0), row(k_k), row(k_a), row(r_k), row(ln_g), row(ln_b),
      w2p, a2p, g2p)


def _compress_kernel(kb_ref, vb_ref, posk_ref, posv_ref, w1k_ref, w1v_ref, w2k_ref, w2v_ref,
                     kc_ref, vc_ref, *, n_cmp):
    half = CMP_STRIDE * HEAD_DIM

    def run(blk_ref, pos_ref, w1_ref, w2_ref):
        a = blk_ref[0, 0]
        lo = _mm(a + pos_ref[0:1, :], w1_ref[0:half, :])
        hi = _mm(a + pos_ref[1:2, :], w1_ref[half:2 * half, :])
        n_half = a.shape[0]
        hid = lo + pltpu.roll(hi, n_half - 1, 0)
        out = _mm(_gelu_tanh(hid), w2_ref[...])
        rows = lax.broadcasted_iota(jnp.int32, out.shape, 0)
        return jnp.where(rows < n_cmp, out, 0.0)

    kc_ref[0, 0] = run(kb_ref, posk_ref, w1k_ref, w2k_ref).astype(BF16)
    vc_ref[0, 0] = run(vb_ref, posv_ref, w1v_ref, w2v_ref).T.astype(BF16)


def _compress(kc_blocks, vc_blocks, pos_k, pos_v, w1_k, w1_v, w2_k, w2_v, n_cmp):
    b, g, nh, hw = kc_blocks.shape
    blk = pl.BlockSpec((1, 1, nh, hw), lambda bi, gi: (bi, gi, 0, 0))
    full = lambda a: pl.BlockSpec(a.shape, lambda bi, gi: (0,) * a.ndim)
    return pl.pallas_call(
        functools.partial(_compress_kernel, n_cmp=n_cmp),
        grid=(b, g),
        in_specs=[blk, blk, full(pos_k), full(pos_v), full(w1_k), full(w1_v), full(w2_k), full(w2_v)],
        out_specs=[pl.BlockSpec((1, 1, nh, HEAD_DIM), lambda bi, gi: (bi, gi, 0, 0)),
                   pl.BlockSpec((1, 1, HEAD_DIM, nh), lambda bi, gi: (bi, gi, 0, 0))],
        out_shape=[jax.ShapeDtypeStruct((b, g, nh, HEAD_DIM), BF16),
                   jax.ShapeDtypeStruct((b, g, HEAD_DIM, nh), BF16)],
        compiler_params=_params("parallel", "parallel"),
        name="nsa_compress",
    )(kc_blocks, vc_blocks, pos_k, pos_v, w1_k, w1_v, w2_k, w2_v)


def _nsa_kernel(slope_ref, q_ref, s3_ref, kc_ref, vc_ref, ks_ref, vs_ref, kw_ref, vw_ref, gl_ref, ovw_ref,
                o_ref, sel_scr, list_scr, valid_scr, sa_scr, sb_scr, m_scr, acc_scr, *, n_rep, n_blk):
    g = pl.program_id(1)
    i = pl.program_id(2)
    t0 = i * TQ
    d = HEAD_DIM
    ck = ATT_CHUNK
    gsz = SEL_GROUP
    bpg = gsz // SEL_LEN
    n_grp = n_blk // bpg
    q_aug = [jnp.concatenate([q_ref[0, 0, r], s3_ref[0, r]], axis=0) for r in range(n_rep)]
    slopes = [slope_ref[g * n_rep + r] for r in range(n_rep)]
    t_row = t0 + lax.broadcasted_iota(jnp.int32, (1, TQ), 1)
    key_in_chunk = lax.broadcasted_iota(jnp.int32, (ck, TQ), 0)
    key_in_blk = lax.broadcasted_iota(jnp.int32, (SEL_LEN, TQ), 0)

    kc = kc_ref[0, 0]
    vc = vc_ref[0, 0]
    s_cmp = [jnp.dot(kc, q_aug[r], preferred_element_type=F32) for r in range(n_rep)]
    n_wc = (NSA_WINDOW + TQ) // ck
    start = pl.multiple_of(jnp.maximum(t0 - NSA_WINDOW, 0), LANES)
    kw_band = kw_ref[0, 0, pl.ds(start, n_wc * ck), :]
    vw_band = vw_ref[0, 0, :, pl.ds(start, n_wc * ck)]
    s_win = [jnp.dot(kw_band, q_aug[r], preferred_element_type=F32) for r in range(n_rep)]

    def score_group(gidx, s_scr):
        k_cat = ks_ref[0, 0, pl.ds(pl.multiple_of(gidx * gsz, gsz), gsz), :]
        for r in range(n_rep):
            s_scr[r] = jnp.dot(k_cat, q_aug[r], preferred_element_type=F32)

    diag_group = (t0 + TQ - 1) // gsz
    score_group(diag_group, sb_scr)

    cmp_end = lax.broadcasted_iota(jnp.int32, (kc.shape[0], TQ), 0) * CMP_STRIDE + (CMP_LEN - 1)
    valid_c = cmp_end <= t_row
    p_sum = None
    o_cmp = []
    for r in range(n_rep):
        s = jnp.where(valid_c, s_cmp[r], NEG_INF)
        m = jnp.max(s, axis=0, keepdims=True)
        m = jnp.where(m == NEG_INF, 0.0, m)
        p = jnp.exp2(s - m)
        den = jnp.sum(p, axis=0, keepdims=True)
        p = p * (1.0 / jnp.where(den > 0, den, 1.0))
        p_sum = p if p_sum is None else p_sum + p
        o_cmp.append(jnp.dot(vc, p.astype(BF16), preferred_element_type=F32))

    o_win = []
    win_valid = []
    for c in range(n_wc):
        dist = t_row - (start + c * ck + key_in_chunk)
        win_valid.append(jnp.logical_and(dist >= 0, dist < NSA_WINDOW))
    for r in range(n_rep):
        parts = [jnp.where(win_valid[c], s_win[r][c * ck:(c + 1) * ck], NEG_INF) for c in range(n_wc)]
        offs = [slopes[r] * (c * ck - _chunk_phase(start, c)).astype(F32) for c in range(n_wc)]
        p, _ = _softmax_cols_chunks(parts, offs)
        pv = jnp.dot(vw_band, p.astype(BF16), preferred_element_type=F32)
        den = pv[d:d + 1]
        o_win.append(pv[0:d] / jnp.where(den > 0, den, 1.0))

    imp = _mm_exact_lhs(ovw_ref[...], p_sum, 3)
    blk = lax.broadcasted_iota(jnp.int32, (n_blk, TQ), 0)
    cur = (t0 + lax.broadcasted_iota(jnp.int32, (n_blk, TQ), 1)) // SEL_LEN
    forced = jnp.logical_or(blk == 0, jnp.logical_or(blk == cur, blk == cur - 1))
    score = jnp.where(forced, jnp.inf, jnp.where(blk > cur, NEG_INF, imp))
    sel = jnp.zeros((n_blk, TQ), F32)
    for _ in range(min(N_SEL, n_blk)):
        best = jnp.max(score, axis=0, keepdims=True)
        first = jnp.min(jnp.where(score == best, blk, n_blk), axis=0, keepdims=True)
        hit = blk == first
        sel = jnp.where(hit, 1.0, sel)
        score = jnp.where(hit, NEG_INF, score)
    sel_scr[...] = sel

    ones = jnp.ones((8, TQ), BF16)
    cnt = _mm_nt(ones, sel)
    pool = (lax.broadcasted_iota(jnp.int32, (n_blk, LANES), 0) // bpg
            == lax.broadcasted_iota(jnp.int32, (n_blk, LANES), 1))
    cnt_g = _mm(cnt, jnp.where(pool, 1.0, 0.0))
    for j in range(n_grp + 1):
        list_scr[j] = 0
        valid_scr[j] = 0
    pos = jnp.int32(n_grp)
    for j in reversed(range(n_grp)):
        hit = jnp.logical_and(cnt_g[0, j] > 0.0, j < diag_group)
        pos = pos - hit.astype(jnp.int32)
        slot = jnp.where(hit, pos, n_grp)
        list_scr[slot] = j
        valid_scr[slot] = 1
    n_pairs = (n_grp - pos + 1) // 2
    first = n_grp - 2 * n_pairs

    m_scr[...] = jnp.full(m_scr.shape, NEG_INF, F32)
    acc_scr[...] = jnp.zeros(acc_scr.shape, F32)

    def reduce_group(gidx, live, s_scr, diagonal):
        k0 = pl.multiple_of(gidx * gsz, gsz)
        rows = sel_scr[pl.ds(pl.multiple_of(gidx * bpg, bpg), bpg), :] * live
        v_cat = vs_ref[0, 0, :, pl.ds(k0, gsz)]
        masks = []
        for u in range(bpg):
            picked = rows[u:u + 1, :] > 0.5
            if diagonal:
                picked = jnp.logical_and(picked, k0 + u * SEL_LEN + key_in_blk <= t_row)
            masks.append(picked)
        rel = (k0 - t0).astype(F32)
        for r in range(n_rep):
            parts = [jnp.where(masks[u], s_scr[r, pl.ds(u * SEL_LEN, SEL_LEN), :], NEG_INF) for u in range(bpg)]
            off = slopes[r] * rel
            top = parts[0]
            for part in parts[1:]:
                top = jnp.maximum(top, part)
            m_old = m_scr[r]
            m_new = jnp.maximum(m_old, jnp.max(top, axis=0, keepdims=True) + off)
            m_safe = jnp.where(m_new == NEG_INF, 0.0, m_new)
            alpha = jnp.exp2(m_old - m_safe)
            sub = m_safe - off
            p = jnp.concatenate([jnp.exp2(part - sub) for part in parts], axis=0)
            acc_scr[r] = alpha * acc_scr[r] + jnp.dot(v_cat, p.astype(BF16), preferred_element_type=F32)
            m_scr[r] = m_new

    score_group(list_scr[first], sa_scr)
    reduce_group(diag_group, 1.0, sb_scr, True)

    def body(pi, carry):
        a = first + 2 * pi
        score_group(list_scr[a + 1], sb_scr)
        reduce_group(list_scr[a], valid_scr[a].astype(F32), sa_scr, False)
        score_group(list_scr[a + 2], sa_scr)
        reduce_group(list_scr[a + 1], valid_scr[a + 1].astype(F32), sb_scr, False)
        return carry

    lax.fori_loop(0, n_pairs, body, 0)

    gates = _sigmoid(gl_ref[0, 0])
    outs = []
    for r in range(n_rep):
        acc = acc_scr[r]
        l_fin = acc[d:d + 1]
        o_slc = acc[0:d] / jnp.where(l_fin > 0, l_fin, 1.0)
        g0 = gates[N_BRANCH * r + 0:N_BRANCH * r + 1, :]
        g1 = gates[N_BRANCH * r + 1:N_BRANCH * r + 2, :]
        g2 = gates[N_BRANCH * r + 2:N_BRANCH * r + 3, :]
        outs.append(g0 * o_cmp[r] + g1 * o_slc + g2 * o_win[r])
    o_ref[0] = jnp.concatenate(outs, axis=0).astype(o_ref.dtype)


def _nsa_attention(slopes, q_t, slope_rows, kc, vc_t, ks, vs_t, kw, vw_t, gl_t, ov_w):
    b, g, r, d, t = q_t.shape
    n_blk = t // SEL_LEN
    n_grp = t // SEL_GROUP
    ncp = kc.shape[2]
    ka = kc.shape[3]
    va = vs_t.shape[2]
    idx4 = lambda bi, gi, i: (bi, gi, 0, 0)
    return pl.pallas_call(
        functools.partial(_nsa_kernel, n_rep=r, n_blk=n_blk),
        grid=(b, g, t // TQ),
        in_specs=[pl.BlockSpec(memory_space=pltpu.SMEM),
                  pl.BlockSpec((1, 1, r, d, TQ), lambda bi, gi, i: (bi, gi, 0, 0, i)),
                  pl.BlockSpec((1, r, ka - d, TQ), lambda bi, gi, i: (gi, 0, 0, 0)),
                  pl.BlockSpec((1, 1, ncp, ka), idx4),
                  pl.BlockSpec((1, 1, d, ncp), idx4),
                  pl.BlockSpec((1, 1, t, ka), idx4),
                  pl.BlockSpec((1, 1, va, t), idx4),
                  pl.BlockSpec((1, 1, t, ka), idx4),
                  pl.BlockSpec((1, 1, va, t), idx4),
                  pl.BlockSpec((1, 1, r * N_BRANCH, TQ), lambda bi, gi, i: (bi, gi, 0, i)),
                  pl.BlockSpec(ov_w.shape, lambda bi, gi, i: (0, 0))],
        out_specs=pl.BlockSpec((1, r * d, TQ), lambda bi, gi, i: (bi, gi, i)),
        out_shape=jax.ShapeDtypeStruct((b, g * r * d, t), BF16),
        scratch_shapes=[pltpu.VMEM((n_blk, TQ), F32),
                        pltpu.SMEM((n_grp + 2,), jnp.int32),
                        pltpu.SMEM((n_grp + 2,), jnp.int32),
                        pltpu.VMEM((r, SEL_GROUP, TQ), F32),
                        pltpu.VMEM((r, SEL_GROUP, TQ), F32),
                        pltpu.VMEM((r, 1, TQ), F32),
                        pltpu.VMEM((r, va, TQ), F32)],
        compiler_params=_params("parallel", "parallel", "arbitrary"),
        name="nsa_attention",
    )(slopes, q_t, slope_rows, kc, vc_t, ks, vs_t, kw, vw_t, gl_t, ov_w)


def _alibi_slopes(n_heads):
    return jnp.exp2(-8.0 * jnp.arange(1, n_heads + 1, dtype=F32) / n_heads) * LOG2E


def _truncation_split_bf16(x, n):
    parts = []
    for _ in range(n):
        top = lax.bitcast_convert_type(lax.bitcast_convert_type(x, jnp.uint32) & jnp.uint32(0xFFFF0000), F32)
        parts.append(top.astype(BF16))
        x = x - top
    return parts


def _slope_query_rows(n_heads, n_groups):
    parts = _truncation_split_bf16(_alibi_slopes(n_heads), 3)
    rows = jnp.zeros((n_heads, KEY_AUG), BF16)
    for j, p in enumerate(parts):
        rows = rows.at[:, j].set(p).at[:, 3 + j].set(p)
    rows = jnp.broadcast_to(rows[:, :, None], (n_heads, KEY_AUG, TQ))
    return rows.reshape(n_groups, n_heads // n_groups, KEY_AUG, TQ)


def _augment_keys(k_tm, lo, hi):
    b, g, n, _ = k_tm.shape
    extra = jnp.zeros((n, KEY_AUG), F32)
    extra = extra.at[:, 0:3].set(lo.astype(F32)[:, None]).at[:, 3:6].set(hi.astype(F32)[:, None]).astype(BF16)
    return jnp.concatenate([k_tm, jnp.broadcast_to(extra, (b, g, n, KEY_AUG))], axis=-1)


def _token_key_positions(t):
    pos = jnp.arange(t) % POS_PERIOD
    lo = pos % ATT_CHUNK
    return lo, pos - lo


def _augment_values(v_fm):
    b, g, _, n = v_fm.shape
    extra = jnp.zeros((VAL_AUG, n), BF16).at[0].set(1.0)
    return jnp.concatenate([v_fm, jnp.broadcast_to(extra, (b, g, VAL_AUG, n))], axis=2)


def _heads_feature_major(a, b, t, g):
    return a.reshape(b, t, g, HEAD_DIM).transpose(0, 2, 3, 1)


def _heads_token_major(a, b, t, g):
    return a.reshape(b, t, g, HEAD_DIM).transpose(0, 2, 1, 3)


def _hybrid_layer(x3d, g_pre, g_post, w_in, w_out, sinks, mu, w0, w2, a0, a2, g2, k_k, k_a,
                  r_k, ln_g, ln_b):
    b, t, d = x3d.shape
    gq, rq = SWA_KV_HEADS, SWA_HEADS // SWA_KV_HEADS
    scale = HEAD_DIM ** -0.5 * LOG2E
    wq_t = (w_in[:, :SWA_Q_COLS] * scale).T.astype(BF16)
    q_fm, kv, z_rwkv = _norm_proj(x3d, g_pre, wq_t, w_in[:, SWA_Q_COLS:SWA_COLS].astype(BF16),
                                  w_in[:, SWA_COLS:].astype(BF16))
    lo, hi = _token_key_positions(t)
    q_t = q_fm.reshape(b, gq, rq, HEAD_DIM, t)
    kv = kv.reshape(b * t, 2 * SWA_KV_COLS)
    k = _augment_keys(_heads_token_major(kv[:, :SWA_KV_COLS], b, t, gq), lo, hi)
    v_t = _augment_values(_heads_feature_major(kv[:, SWA_KV_COLS:], b, t, gq))
    o_a = _swa_attention(_alibi_slopes(SWA_HEADS), sinks.astype(F32), q_t, _slope_query_rows(SWA_HEADS, gq), k, v_t)
    o_b = _rwkv_time_mix(z_rwkv, mu, w0, w2, a0, a2, g2, k_k, k_a, r_k.reshape(-1), ln_g, ln_b)
    w_out = w_out.astype(BF16)
    return _out_proj([o_a, o_b], [True, False], [w_out[:SWA_Q_COLS], w_out[SWA_Q_COLS:]], x3d, g_post)


def _overlap_weights(n_blk, n_cmp_pad, n_cmp):
    ratio, span = SEL_LEN // CMP_STRIDE, CMP_LEN // CMP_STRIDE
    w = np.zeros((n_blk, n_cmp_pad), np.float32)
    for j in range(n_blk):
        for a in range(ratio):
            for s in range(span):
                c = ratio * j + a - s
                if 0 <= c < n_cmp:
                    w[j, c] += 1.0
    return jnp.asarray(w, BF16)


def _nsa_layer(x3d, g_pre, g_post, w_in, w_out, pos_k, w1_k, w2_k, pos_v, w1_v, w2_v):
    b, t, d = x3d.shape
    g, r = NSA_KV_HEADS, NSA_HEADS // NSA_KV_HEADS
    off = NSA_Q_COLS
    kv = NSA_KV_COLS
    scale = HEAD_DIM ** -0.5 * LOG2E
    wq_t = (w_in[:, :off] * scale).T.astype(BF16)
    w_f = jnp.concatenate([w_in[:, off:off + 2 * kv], w_in[:, off + 6 * kv:]], axis=1)
    w_f = jnp.pad(w_f, ((0, 0), (0, (-w_f.shape[1]) % LANES))).astype(BF16)
    q_fm, z_b, z_f = _norm_proj(x3d, g_pre, wq_t, w_in[:, off + 2 * kv:off + 6 * kv].astype(BF16), w_f)
    z_b = z_b.reshape(b * t, 4 * kv)
    z_f = z_f.reshape(b * t, -1)
    cols = lambda j: z_f[:, j * kv:(j + 1) * kv] if j < 2 else z_b[:, (j - 2) * kv:(j - 1) * kv]
    q_t = q_fm.reshape(b, g, r, HEAD_DIM, t)
    n_half = t // CMP_STRIDE
    n_cmp = (t - CMP_LEN) // CMP_STRIDE + 1
    half_rows = lambda a: a.reshape(b, n_half, CMP_STRIDE, g, HEAD_DIM).transpose(0, 3, 1, 2, 4).reshape(
        b, g, n_half, CMP_STRIDE * HEAD_DIM)
    flat_pos = lambda p: p.reshape(CMP_LEN // CMP_STRIDE, CMP_STRIDE * HEAD_DIM)
    flat_w1 = lambda w: w.reshape(CMP_LEN * HEAD_DIM, CMP_HIDDEN).astype(BF16)
    kc, vc_t = _compress(half_rows(cols(0)), half_rows(cols(1)), flat_pos(pos_k), flat_pos(pos_v),
                         flat_w1(w1_k), flat_w1(w1_v), w2_k.astype(BF16), w2_v.astype(BF16), n_cmp)
    lo, hi = _token_key_positions(t)
    cmp_pos = jnp.arange(n_half) * CMP_STRIDE
    kc = _augment_keys(kc, cmp_pos, jnp.zeros_like(cmp_pos))
    ks = _augment_keys(_heads_token_major(cols(2).astype(BF16), b, t, g), lo, hi)
    vs_t = _augment_values(_heads_feature_major(cols(3).astype(BF16), b, t, g))
    kw = _augment_keys(_heads_token_major(cols(4).astype(BF16), b, t, g), lo, hi)
    vw_t = _augment_values(_heads_feature_major(cols(5).astype(BF16), b, t, g))
    gl_t = z_f[:, 2 * kv:2 * kv + NSA_HEADS * N_BRANCH].reshape(b, t, g, r * N_BRANCH).transpose(0, 2, 3, 1)
    ov_w = _overlap_weights(t // SEL_LEN, n_half, n_cmp)
    o = _nsa_attention(_alibi_slopes(NSA_HEADS), q_t, _slope_query_rows(NSA_HEADS, g), kc, vc_t, ks, vs_t,
                       kw, vw_t, gl_t, ov_w)
    return _out_proj([o], [True], [w_out.astype(BF16)], x3d, g_post)


def kernel(x, mix_pre_g, mix_post_g, ffn_pre_g, ffn_post_g, hy_w_in, hy_w_out, swa_sinks, rwkv_mu, rwkv_w0, rwkv_w2, rwkv_a0, rwkv_a2, rwkv_g2, rwkv_k_k, rwkv_k_a, rwkv_r_k, rwkv_ln_g, rwkv_ln_b, nsa_w_in, nsa_w_out, nsa_cmp_pos_k, nsa_cmp_w1_k, nsa_cmp_w2_k, nsa_cmp_pos_v, nsa_cmp_w1_v, nsa_cmp_w2_v, ffn_w_up, ffn_conv_w, ffn_conv_b, ffn_w_down):
    b, t, d = x.shape
    depth = mix_pre_g.shape[0]
    for layer in range(depth):
        i = layer // 2
        if layer % 2 == 0:
            x = _hybrid_layer(x, mix_pre_g[layer], mix_post_g[layer], hy_w_in[i], hy_w_out[i],
                              swa_sinks[i], rwkv_mu[i], rwkv_w0[i], rwkv_w2[i], rwkv_a0[i], rwkv_a2[i],
                              rwkv_g2[i], rwkv_k_k[i], rwkv_k_a[i], rwkv_r_k[i], rwkv_ln_g[i], rwkv_ln_b[i])
        else:
            x = _nsa_layer(x, mix_pre_g[layer], mix_post_g[layer], nsa_w_in[i], nsa_w_out[i],
                           nsa_cmp_pos_k[i], nsa_cmp_w1_k[i], nsa_cmp_w2_k[i],
                           nsa_cmp_pos_v[i], nsa_cmp_w1_v[i], nsa_cmp_w2_v[i])
        x = _conv_ffn(x.reshape(b * t, d), t, ffn_pre_g[layer], ffn_w_up[layer].astype(BF16), ffn_conv_w[layer],
                      ffn_conv_b[layer], ffn_w_down[layer].astype(BF16), ffn_post_g[layer]).reshape(b, t, d)
    return x
```

```python
import functools
import math

import numpy as np
import jax
import jax.numpy as jnp
from jax import lax
from jax.experimental import pallas as pl
from jax.experimental.pallas import tpu as pltpu

F32 = jnp.float32
BF16 = jnp.bfloat16

D_MODEL = 1024
HEAD_DIM = 64
SWA_HEADS = 8
SWA_KV_HEADS = 2
SWA_WINDOW = 128
RWKV_HEADS = 8
RWKV_DIM = RWKV_HEADS * HEAD_DIM
RWKV_W_LORA = 64
RWKV_A_LORA = 64
RWKV_G_LORA = 128
RWKV_GN_EPS = 64e-5
NSA_HEADS = 16
NSA_KV_HEADS = 4
CMP_LEN = 32
CMP_STRIDE = 16
CMP_HIDDEN = 256
SEL_LEN = 64
N_SEL = 8
NSA_WINDOW = 256
N_BRANCH = 3
D_FF = 2816
CONV_WIDTH = 3
NORM_EPS = 1e-6

SWA_Q_COLS = SWA_HEADS * HEAD_DIM
SWA_KV_COLS = SWA_KV_HEADS * HEAD_DIM
SWA_COLS = SWA_Q_COLS + 2 * SWA_KV_COLS
NSA_KV_COLS = NSA_KV_HEADS * HEAD_DIM
NSA_Q_COLS = NSA_HEADS * HEAD_DIM

LANES = 128
VMEM_LIMIT = 56 * 1024 * 1024

TQ = 256
ATT_CHUNK = 128
POS_PERIOD = 512
KEY_AUG = 16
VAL_AUG = 16
SEL_GROUP = POS_PERIOD
RWKV_CHUNK = 64
RWKV_QUAD = 4
RWKV_BATCH = 8
NEG_INF = float("-inf")
LOG2E = math.log2(math.e)


def _params(*sem):
    return pltpu.CompilerParams(dimension_semantics=sem, vmem_limit_bytes=VMEM_LIMIT)


def _rms(x, g):
    return x * lax.rsqrt(jnp.mean(x * x, axis=-1, keepdims=True) + NORM_EPS) * g


def _mm(a, b):
    return jnp.dot(a.astype(BF16), b.astype(BF16), preferred_element_type=F32)


def _mm_nt(a, b):
    return lax.dot_general(a.astype(BF16), b.astype(BF16), (((1,), (1,)), ((), ())),
                           preferred_element_type=F32)


def _mm_tn(a, b):
    return lax.dot_general(a.astype(BF16), b.astype(BF16), (((0,), (0,)), ((), ())),
                           preferred_element_type=F32)


def _split_bf16(x, n):
    parts = []
    for _ in range(n):
        p = x.astype(BF16)
        parts.append(p)
        x = x - p.astype(F32)
    return parts


def _mm_exact_lhs(a_bf16, x, n):
    out = None
    for p in _split_bf16(x, n):
        t = jnp.dot(a_bf16, p, preferred_element_type=F32)
        out = t if out is None else out + t
    return out


def _mm_exact_rhs(x, b_bf16, n):
    out = None
    for p in _split_bf16(x, n):
        t = jnp.dot(p, b_bf16, preferred_element_type=F32)
        out = t if out is None else out + t
    return out


def _sigmoid(x):
    return 1.0 / (1.0 + jnp.exp(-x))


def _gelu_tanh(x):
    c = math.sqrt(2.0 / math.pi)
    return 0.5 * x * (1.0 + jnp.tanh(c * (x + 0.044715 * (x * x * x))))


def _norm_proj_kernel(*refs, n_fm):
    x_ref, g_ref = refs[0:2]
    wfm_refs = refs[2:2 + 2 * n_fm]
    wb_ref, wf_ref = refs[2 + 2 * n_fm:4 + 2 * n_fm]
    out_refs = refs[4 + 2 * n_fm:]
    h = _rms(x_ref[0], g_ref[...]).astype(BF16)
    for j in range(n_fm):
        w_ref, c_ref, o_ref = wfm_refs[2 * j], wfm_refs[2 * j + 1], out_refs[j]
        o_ref[0] = (lax.dot_general(w_ref[...], h, (((1,), (1,)), ((), ())), preferred_element_type=F32)
                    + c_ref[...]).astype(o_ref.dtype)
    out_refs[n_fm][0] = jnp.dot(h, wb_ref[...], preferred_element_type=F32).astype(out_refs[n_fm].dtype)
    out_refs[n_fm + 1][0] = jnp.dot(h, wf_ref[...], preferred_element_type=F32)


def _norm_proj(x3d, g, fm_weights, w_b, w_f, tm=512):
    b, t, d = x3d.shape
    cb, cf = w_b.shape[1], w_f.shape[1]
    const = lambda bi, j: (0, 0)
    fm_specs, fm_args, fm_outs, fm_shapes = [], [], [], []
    for w_t, c in fm_weights:
        f = w_t.shape[0]
        fm_specs += [pl.BlockSpec((f, d), const), pl.BlockSpec((f, 1), const)]
        fm_args += [w_t, c]
        fm_outs.append(pl.BlockSpec((1, f, tm), lambda bi, j: (bi, 0, j)))
        fm_shapes.append(jax.ShapeDtypeStruct((b, f, t), BF16))
    return pl.pallas_call(
        functools.partial(_norm_proj_kernel, n_fm=len(fm_weights)),
        grid=(b, t // tm),
        in_specs=[pl.BlockSpec((1, tm, d), lambda bi, j: (bi, j, 0)), pl.BlockSpec((1, d), const)]
        + fm_specs + [pl.BlockSpec((d, cb), const), pl.BlockSpec((d, cf), const)],
        out_specs=fm_outs + [pl.BlockSpec((1, tm, cb), lambda bi, j: (bi, j, 0)),
                             pl.BlockSpec((1, tm, cf), lambda bi, j: (bi, j, 0))],
        out_shape=fm_shapes + [jax.ShapeDtypeStruct((b, t, cb), BF16),
                               jax.ShapeDtypeStruct((b, t, cf), F32)],
        compiler_params=_params("parallel", "parallel"),
        name="norm_proj",
    )(x3d, g.reshape(1, d), *fm_args, w_b, w_f)


def _value_weights(w_v, n_groups):
    d = w_v.shape[0]
    w_t = w_v.T.reshape(n_groups, HEAD_DIM, d)
    w_t = jnp.concatenate([w_t, jnp.zeros((n_groups, VAL_AUG, d), w_t.dtype)], axis=1)
    ones_row = jnp.zeros((n_groups, HEAD_DIM + VAL_AUG, 1), F32).at[:, HEAD_DIM].set(1.0)
    return (w_t.reshape(n_groups * (HEAD_DIM + VAL_AUG), d).astype(BF16),
            ones_row.reshape(n_groups * (HEAD_DIM + VAL_AUG), 1))


def _out_proj_kernel(*refs, feature_major):
    n_parts = len(feature_major)
    o_refs = refs[:n_parts]
    w_refs = refs[n_parts:2 * n_parts]
    x_ref, g_ref, y_ref = refs[2 * n_parts:]
    m = None
    for o_ref, w_ref, fm in zip(o_refs, w_refs, feature_major):
        if fm:
            t = lax.dot_general(o_ref[0].astype(BF16), w_ref[...], (((0,), (0,)), ((), ())),
                                preferred_element_type=F32)
        else:
            t = jnp.dot(o_ref[0].astype(BF16), w_ref[...], preferred_element_type=F32)
        m = t if m is None else m + t
    y_ref[0] = x_ref[0] + _rms(m, g_ref[...])


def _out_proj(parts, feature_major, weights, x3d, g, tm=512):
    b, t, d = x3d.shape
    in_specs = []
    for p, fm in zip(parts, feature_major):
        if fm:
            in_specs.append(pl.BlockSpec((1, p.shape[1], tm), lambda bi, j: (bi, 0, j)))
        else:
            in_specs.append(pl.BlockSpec((1, tm, p.shape[2]), lambda bi, j: (bi, j, 0)))
    in_specs += [pl.BlockSpec(w.shape, lambda bi, j: (0, 0)) for w in weights]
    in_specs += [pl.BlockSpec((1, tm, d), lambda bi, j: (bi, j, 0)), pl.BlockSpec((1, d), lambda bi, j: (0, 0))]
    return pl.pallas_call(
        functools.partial(_out_proj_kernel, feature_major=tuple(feature_major)),
        grid=(b, t // tm),
        in_specs=in_specs,
        out_specs=pl.BlockSpec((1, tm, d), lambda bi, j: (bi, j, 0)),
        out_shape=jax.ShapeDtypeStruct((b, t, d), F32),
        compiler_params=_params("parallel", "parallel"),
        name="out_proj",
    )(*parts, *weights, x3d, g.reshape(1, d))


FFN_HALO = 16


def _ffn_kernel(xh_ref, x_ref, gpre_ref, wg_ref, wv_ref, cw_ref, cb_ref, wd_ref, gpost_ref,
                y_ref, h_scr, acc_scr, *, tm, tiles_per_seq):
    i = pl.program_id(0)
    j = pl.program_id(1)
    nj = pl.num_programs(1)

    @pl.when(j == 0)
    def _():
        xe = jnp.concatenate([xh_ref[...], x_ref[...]], axis=0)
        h_scr[...] = _rms(xe, gpre_ref[...]).astype(BF16)

    h = h_scr[...]
    gate = jnp.dot(h, wg_ref[...], preferred_element_type=F32)
    val = jnp.dot(h[FFN_HALO:], wv_ref[...], preferred_element_type=F32)
    row = lax.broadcasted_iota(jnp.int32, gate.shape, 0)
    seq_start = (i % tiles_per_seq) == 0
    gate = jnp.where(jnp.logical_and(row < FFN_HALO, seq_start), 0.0, gate)
    cw = cw_ref[...]
    conv = cb_ref[...] + gate[FFN_HALO:] * cw[2:3]
    conv = conv + gate[FFN_HALO - 2:-2] * cw[0:1]
    conv = conv + gate[FFN_HALO - 1:-1] * cw[1:2]
    act = _gelu_tanh(conv) * val
    part = jnp.dot(act.astype(BF16), wd_ref[...], preferred_element_type=F32)

    @pl.when(j == 0)
    def _():
        acc_scr[...] = part

    @pl.when(j > 0)
    def _():
        acc_scr[...] += part

    @pl.when(j == nj - 1)
    def _():
        y_ref[...] = x_ref[...] + _rms(acc_scr[...], gpost_ref[...])


def _conv_ffn(x2d, seq_len, g_pre, w_up, conv_w, conv_b, w_down, g_post, tm=512, nf=2):
    n, d = x2d.shape
    tf = D_FF // nf
    hb = tm // FFN_HALO
    return pl.pallas_call(
        functools.partial(_ffn_kernel, tm=tm, tiles_per_seq=seq_len // tm),
        grid=(n // tm, nf),
        in_specs=[pl.BlockSpec((FFN_HALO, d), lambda i, j: (jnp.maximum(i * hb - 1, 0), 0)),
                  pl.BlockSpec((tm, d), lambda i, j: (i, 0)),
                  pl.BlockSpec((1, d), lambda i, j: (0, 0)),
                  pl.BlockSpec((d, tf), lambda i, j: (0, j)),
                  pl.BlockSpec((d, tf), lambda i, j: (0, nf + j)),
                  pl.BlockSpec((CONV_WIDTH, tf), lambda i, j: (0, j)),
                  pl.BlockSpec((1, tf), lambda i, j: (0, j)),
                  pl.BlockSpec((tf, d), lambda i, j: (j, 0)),
                  pl.BlockSpec((1, d), lambda i, j: (0, 0))],
        out_specs=pl.BlockSpec((tm, d), lambda i, j: (i, 0)),
        out_shape=jax.ShapeDtypeStruct((n, d), F32),
        scratch_shapes=[pltpu.VMEM((tm + FFN_HALO, d), BF16), pltpu.VMEM((tm, d), F32)],
        compiler_params=_params("parallel", "arbitrary"),
        name="conv_ffn",
    )(x2d, x2d, g_pre.reshape(1, d), w_up, w_up, conv_w, conv_b.reshape(1, D_FF), w_down,
      g_post.reshape(1, d))


def _chunk_phase(start, c):
    return ((start // ATT_CHUNK + c) % (POS_PERIOD // ATT_CHUNK)) * ATT_CHUNK


def _softmax_cols_chunks(parts, offs, extra=None):
    m = extra
    for part, off in zip(parts, offs):
        cm = jnp.max(part, axis=0, keepdims=True) + off
        m = cm if m is None else jnp.maximum(m, cm)
    m = jnp.where(m == NEG_INF, 0.0, m)
    p = jnp.concatenate([jnp.exp2(part - (m - off)) for part, off in zip(parts, offs)], axis=0)
    return p, m


def _swa_kernel(slope_ref, sink_ref, q_ref, s3_ref, k_ref, v_ref, o_ref, *, n_rep, window):
    g = pl.program_id(1)
    i = pl.program_id(2)
    t0 = i * TQ
    d = HEAD_DIM
    ck = ATT_CHUNK
    n_c = (window + TQ) // ck
    start = pl.multiple_of(jnp.maximum(t0 - window, 0), LANES)
    kb = k_ref[0, 0, pl.ds(start, n_c * ck), :]
    vb = v_ref[0, 0, :, pl.ds(start, n_c * ck)]
    t_row = t0 + lax.broadcasted_iota(jnp.int32, (1, TQ), 1)
    key_in_chunk = lax.broadcasted_iota(jnp.int32, (ck, TQ), 0)
    valid = []
    for c in range(n_c):
        dist = t_row - (start + c * ck + key_in_chunk)
        valid.append(jnp.logical_and(dist >= 0, dist < window))
    rel_t = (t_row - start).astype(F32)
    scores = []
    for r in range(n_rep):
        q_aug = jnp.concatenate([q_ref[0, 0, r], s3_ref[0, r]], axis=0)
        scores.append(jnp.dot(kb, q_aug, preferred_element_type=F32))
    probs, tails = [], []
    for r in range(n_rep):
        slope = slope_ref[g * n_rep + r]
        parts = [jnp.where(valid[c], scores[r][c * ck:(c + 1) * ck], NEG_INF) for c in range(n_c)]
        offs = [slope * (c * ck - _chunk_phase(start, c)).astype(F32) for c in range(n_c)]
        sink = sink_ref[g * n_rep + r] * LOG2E + slope * rel_t
        p, m = _softmax_cols_chunks(parts, offs, extra=sink)
        probs.append(p.astype(BF16))
        tails.append(jnp.exp2(sink - m))
    outs = []
    for r in range(n_rep):
        pv = jnp.dot(vb, probs[r], preferred_element_type=F32)
        outs.append(pv[0:d] / (pv[d:d + 1] + tails[r]))
    o_ref[0] = jnp.concatenate(outs, axis=0).astype(o_ref.dtype)


def _swa_attention(slopes, sinks, q_t, slope_rows, k, v_t):
    b, g, r, d, t = q_t.shape
    ka, va = k.shape[3], v_t.shape[2]
    smem = pl.BlockSpec(memory_space=pltpu.SMEM)
    return pl.pallas_call(
        functools.partial(_swa_kernel, n_rep=r, window=SWA_WINDOW),
        grid=(b, g, t // TQ),
        in_specs=[smem, smem,
                  pl.BlockSpec((1, 1, r, d, TQ), lambda bi, gi, i: (bi, gi, 0, 0, i)),
                  pl.BlockSpec((1, r, ka - d, TQ), lambda bi, gi, i: (gi, 0, 0, 0)),
                  pl.BlockSpec((1, 1, t, ka), lambda bi, gi, i: (bi, gi, 0, 0)),
                  pl.BlockSpec((1, 1, va, t), lambda bi, gi, i: (bi, gi, 0, 0))],
        out_specs=pl.BlockSpec((1, r * d, TQ), lambda bi, gi, i: (bi, gi, i)),
        out_shape=jax.ShapeDtypeStruct((b, g * r * d, t), BF16),
        compiler_params=_params("parallel", "parallel", "parallel"),
        name="swa_attention",
    )(slopes, sinks, q_t, slope_rows, k, v_t)


def _rwkv_kernel(zr_ref, zk_ref, zv_ref, zl_ref, mur_ref, muk_ref, muv_ref, mul_ref,
                 w0_ref, a0_ref, kk_ref, ka_ref, rk_ref, lng_ref, lnb_ref,
                 w2_ref, a2_ref, g2_ref, o_ref, prev_scr, s_scr, *, nb):
    c = pl.program_id(2)
    cs = RWKV_CHUNK
    wd = RWKV_QUAD * HEAD_DIM
    each = lambda f, *cols: [f(*args) for args in zip(*cols)]

    @pl.when(c == 0)
    def _():
        prev_scr[...] = jnp.zeros_like(prev_scr)
        s_scr[...] = jnp.zeros_like(s_scr)

    row = lax.broadcasted_iota(jnp.int32, (cs, wd), 0)

    def shifted(z_ref, mu_ref, idx):
        outs = []
        for bb in range(nb):
            z = z_ref[bb]
            prev = prev_scr[idx * nb + bb, 0:1, :]
            zp = jnp.where(row == 0, prev, pltpu.roll(z, 1, 0))
            prev_scr[idx * nb + bb, 0:1, :] = z[cs - 1:cs, :]
            outs.append(z + (zp - z) * mu_ref[...])
        return outs

    r = shifted(zr_ref, mur_ref, 0)
    k = shifted(zk_ref, muk_ref, 1)
    v = shifted(zv_ref, muv_ref, 2)
    lo = shifted(zl_ref, mul_ref, 3)

    yw = each(lambda x: w0_ref[...] + _mm(jnp.tanh(x), w2_ref[...]), lo)
    ya = each(lambda x: a0_ref[...] + _mm(x, a2_ref[...]), lo)
    gate = each(lambda x: _mm(_sigmoid(x), g2_ref[...]), lo)
    softplus = each(lambda x: jnp.maximum(-x, 0.0) + jnp.log(1.0 + jnp.exp(-jnp.abs(x))), yw)
    logd = each(lambda x: -jnp.exp(-x - 0.5), softplus)
    a = each(_sigmoid, ya)

    i0 = lax.broadcasted_iota(jnp.int32, (wd, wd), 0)
    i1 = lax.broadcasted_iota(jnp.int32, (wd, wd), 1)
    same_head = (i0 // HEAD_DIM) == (i1 // HEAD_DIM)
    ones_head = jnp.where(same_head, 1.0, 0.0).astype(BF16)

    def head_sum(x):
        return _mm_exact_rhs(x, ones_head, 2)

    kk = each(lambda x: x * kk_ref[...], k)
    kk_ss = each(lambda x: head_sum(x * x), kk)
    kk = each(lambda x, ss: x / jnp.maximum(jnp.sqrt(ss), 1e-12), kk, kk_ss)
    kmod = each(lambda x, y: x * (1.0 + (y - 1.0) * ka_ref[...]), k, a)
    b = each(lambda x, y: x * y, kk, a)

    c0 = lax.broadcasted_iota(jnp.int32, (cs, cs), 0)
    c1 = lax.broadcasted_iota(jnp.int32, (cs, cs), 1)
    tri = jnp.where(c0 >= c1, 1.0, 0.0).astype(BF16)
    cl = each(lambda x: _mm_exact_lhs(tri, x, 3), logd)
    cl_end = each(lambda x: x[cs - 1:cs, :], cl)
    w_inc = each(jnp.exp, cl)
    w_exc = each(lambda x, y: jnp.exp(x - y), cl, logd)
    w_inv = each(lambda x: jnp.exp(-x), cl)
    w_rem = each(lambda x, y: jnp.exp(y - x), cl, cl_end)

    def stack(x):
        return jnp.where(same_head, jnp.concatenate([x] * RWKV_QUAD, axis=0), 0.0)

    mul_stack = lambda x, y: stack(x * y)
    x_kk = each(mul_stack, kk, w_exc)
    x_r = each(mul_stack, r, w_inc)
    x_b = each(mul_stack, b, w_inv)
    x_k = each(mul_stack, kmod, w_inv)
    x_v = each(stack, v)
    x_bd = each(mul_stack, b, w_rem)
    x_kd = each(mul_stack, kmod, w_rem)

    strict = (i0 % cs) > (i1 % cs)
    incl = (i0 % cs) >= (i1 % cs)
    a_ub = each(lambda x, y: jnp.where(strict, _mm_nt(x, y), 0.0), x_kk, x_b)
    a_uk = each(lambda x, y: jnp.where(strict, _mm_nt(x, y), 0.0), x_kk, x_k)
    a_rb = each(lambda x, y: jnp.where(incl, _mm_nt(x, y), 0.0), x_r, x_b)
    a_rk = each(lambda x, y: jnp.where(incl, _mm_nt(x, y), 0.0), x_r, x_k)

    eye = i0 == i1
    pw = a_ub
    t_inv = each(lambda x: jnp.where(eye, 1.0, 0.0) - x, a_ub)
    for _ in range(int(math.log2(cs)) - 1):
        pw = each(lambda x: _mm(x, x), pw)
        t_inv = each(lambda t, x: t + _mm(t, x), t_inv, pw)

    p1 = each(_mm, t_inv, x_kk)
    av = each(_mm, a_uk, x_v)
    p2 = each(_mm, t_inv, av)
    q1 = each(lambda x, y, z: x - _mm(y, z), x_r, a_rb, p1)
    q2 = each(lambda w, x, y, z: _mm(w, x) - _mm(y, z), a_rk, x_v, a_rb, p2)
    g_mat = each(lambda e, x, y: jnp.where(eye, jnp.exp(e), 0.0) - _mm_tn(x, y), cl_end, x_bd, p1)
    h_mat = each(lambda w, x, y, z: _mm_tn(w, x) - _mm_tn(y, z), x_kd, x_v, x_bd, p2)

    s0 = [s_scr[bb] for bb in range(nb)]
    ys = each(lambda x, s, y: _mm(x, s) + y, q1, s0, q2)
    s1 = each(lambda x, s, y: _mm(x, s) + y, g_mat, s0, h_mat)
    for bb in range(nb):
        s_scr[bb] = s1[bb]

    def fold(x):
        y = x[0:cs]
        for hd in range(1, RWKV_QUAD):
            y = y + x[hd * cs:(hd + 1) * cs]
        return y

    y = each(fold, ys)

    inv_n = 1.0 / HEAD_DIM
    mean = each(lambda x: head_sum(x) * inv_n, y)
    yc = each(lambda x, m: x - m, y, mean)
    var = each(lambda x: head_sum(x * x) * inv_n, yc)
    yn = each(lambda x, s: x * lax.rsqrt(s + RWKV_GN_EPS) * lng_ref[...] + lnb_ref[...], yc, var)
    bonus = each(lambda x, y, z: head_sum(x * y * rk_ref[...]) * z, r, kmod, v)
    for bb in range(nb):
        o_ref[bb] = (yn[bb] + bonus[bb]) * gate[bb]


def _rwkv_time_mix(z3, mu, w0, w2, a0, a2, g2, k_k, k_a, r_k, ln_g, ln_b):
    b, t, _ = z3.shape
    wd = RWKV_QUAD * HEAD_DIM
    nq = RWKV_DIM // wd
    base = 0
    lora_rows = RWKV_W_LORA + RWKV_A_LORA + RWKV_G_LORA
    assert lora_rows == wd

    def pad_rows(w, start):
        return jnp.zeros((wd, RWKV_DIM), F32).at[start:start + w.shape[0]].set(w).astype(BF16)

    w2p = pad_rows(w2, 0)
    a2p = pad_rows(a2, RWKV_W_LORA)
    g2p = pad_rows(g2, RWKV_W_LORA + RWKV_A_LORA)
    mu2 = mu.reshape(1, -1)
    row = lambda x: x.reshape(1, RWKV_DIM)
    nb = RWKV_BATCH if b % RWKV_BATCH == 0 else 1
    zspec = lambda off: pl.BlockSpec((nb, RWKV_CHUNK, wd), lambda bi, qi, ci: (bi, ci, base + off + qi))
    zlspec = pl.BlockSpec((nb, RWKV_CHUNK, wd), lambda bi, qi, ci: (bi, ci, base + 3 * nq))
    muspec = lambda off: pl.BlockSpec((1, wd), lambda bi, qi, ci: (0, off + qi))
    mulspec = pl.BlockSpec((1, wd), lambda bi, qi, ci: (0, 3 * nq))
    pspec = pl.BlockSpec((1, wd), lambda bi, qi, ci: (0, qi))
    wspec = pl.BlockSpec((wd, wd), lambda bi, qi, ci: (0, qi))
    return pl.pallas_call(
        functools.partial(_rwkv_kernel, nb=nb),
        grid=(b // nb, nq, t // RWKV_CHUNK),
        in_specs=[zspec(0), zspec(nq), zspec(2 * nq), zlspec,
                  muspec(0), muspec(nq), muspec(2 * nq), mulspec,
                  pspec, pspec, pspec, pspec, pspec, pspec, pspec,
                  wspec, wspec, wspec],
        out_specs=pl.BlockSpec((nb, RWKV_CHUNK, wd), lambda bi, qi, ci: (bi, ci, qi)),
        out_shape=jax.ShapeDtypeStruct((b, t, RWKV_DIM), F32),
        scratch_shapes=[pltpu.VMEM((4 * nb, 8, wd), F32), pltpu.VMEM((nb, wd, wd), F32)],
        compiler_params=_params("parallel", "parallel", "arbitrary"),
        name="rwkv7_time_mix",
    )(z3, z3, z3, z3, mu2, mu2, mu2, mu2,
      row(w0), row(a0), row(k_k), row(k_a), row(r_k), row(ln_g), row(ln_b),
      w2p, a2p, g2p)


def _compress_kernel(kb_ref, vb_ref, posk_ref, posv_ref, w1k_ref, w1v_ref, w2k_ref, w2v_ref,
                     kc_ref, vc_ref, *, n_cmp):
    half = CMP_STRIDE * HEAD_DIM

    def run(blk_ref, pos_ref, w1_ref, w2_ref):
        a = blk_ref[0, 0]
        lo = _mm(a + pos_ref[0:1, :], w1_ref[0:half, :])
        hi = _mm(a + pos_ref[1:2, :], w1_ref[half:2 * half, :])
        n_half = a.shape[0]
        hid = lo + pltpu.roll(hi, n_half - 1, 0)
        out = _mm(_gelu_tanh(hid), w2_ref[...])
        rows = lax.broadcasted_iota(jnp.int32, out.shape, 0)
        return jnp.where(rows < n_cmp, out, 0.0)

    kc_ref[0, 0] = run(kb_ref, posk_ref, w1k_ref, w2k_ref).astype(BF16)
    vc_ref[0, 0] = run(vb_ref, posv_ref, w1v_ref, w2v_ref).T.astype(BF16)


def _compress(kc_blocks, vc_blocks, pos_k, pos_v, w1_k, w1_v, w2_k, w2_v, n_cmp):
    b, g, nh, hw = kc_blocks.shape
    blk = pl.BlockSpec((1, 1, nh, hw), lambda bi, gi: (bi, gi, 0, 0))
    full = lambda a: pl.BlockSpec(a.shape, lambda bi, gi: (0,) * a.ndim)
    return pl.pallas_call(
        functools.partial(_compress_kernel, n_cmp=n_cmp),
        grid=(b, g),
        in_specs=[blk, blk, full(pos_k), full(pos_v), full(w1_k), full(w1_v), full(w2_k), full(w2_v)],
        out_specs=[pl.BlockSpec((1, 1, nh, HEAD_DIM), lambda bi, gi: (bi, gi, 0, 0)),
                   pl.BlockSpec((1, 1, HEAD_DIM, nh), lambda bi, gi: (bi, gi, 0, 0))],
        out_shape=[jax.ShapeDtypeStruct((b, g, nh, HEAD_DIM), BF16),
                   jax.ShapeDtypeStruct((b, g, HEAD_DIM, nh), BF16)],
        compiler_params=_params("parallel", "parallel"),
        name="nsa_compress",
    )(kc_blocks, vc_blocks, pos_k, pos_v, w1_k, w1_v, w2_k, w2_v)


def _nsa_kernel(slope_ref, q_ref, s3_ref, kc_ref, vc_ref, ks_ref, vs_ref, kw_ref, vw_ref, gl_ref, ovw_ref,
                o_ref, sel_scr, list_scr, valid_scr, sa_scr, sb_scr, m_scr, acc_scr, *, n_rep, n_blk):
    g = pl.program_id(1)
    i = pl.program_id(2)
    t0 = i * TQ
    d = HEAD_DIM
    ck = ATT_CHUNK
    gsz = SEL_GROUP
    bpg = gsz // SEL_LEN
    n_grp = n_blk // bpg
    q_aug = [jnp.concatenate([q_ref[0, 0, r], s3_ref[0, r]], axis=0) for r in range(n_rep)]
    slopes = [slope_ref[g * n_rep + r] for r in range(n_rep)]
    t_row = t0 + lax.broadcasted_iota(jnp.int32, (1, TQ), 1)
    key_in_chunk = lax.broadcasted_iota(jnp.int32, (ck, TQ), 0)
    key_in_blk = lax.broadcasted_iota(jnp.int32, (SEL_LEN, TQ), 0)

    kc = kc_ref[0, 0]
    vc = vc_ref[0, 0]
    s_cmp = [jnp.dot(kc, q_aug[r], preferred_element_type=F32) for r in range(n_rep)]
    n_wc = (NSA_WINDOW + TQ) // ck
    start = pl.multiple_of(jnp.maximum(t0 - NSA_WINDOW, 0), LANES)
    kw_band = kw_ref[0, 0, pl.ds(start, n_wc * ck), :]
    vw_band = vw_ref[0, 0, :, pl.ds(start, n_wc * ck)]
    s_win = [jnp.dot(kw_band, q_aug[r], preferred_element_type=F32) for r in range(n_rep)]

    def score_group(gidx, s_scr):
        k_cat = ks_ref[0, 0, pl.ds(pl.multiple_of(gidx * gsz, gsz), gsz), :]
        for r in range(n_rep):
            s_scr[r] = jnp.dot(k_cat, q_aug[r], preferred_element_type=F32)

    diag_group = (t0 + TQ - 1) // gsz
    score_group(diag_group, sb_scr)

    cmp_end = lax.broadcasted_iota(jnp.int32, (kc.shape[0], TQ), 0) * CMP_STRIDE + (CMP_LEN - 1)
    valid_c = cmp_end <= t_row
    p_sum = None
    o_cmp = []
    for r in range(n_rep):
        s = jnp.where(valid_c, s_cmp[r], NEG_INF)
        m = jnp.max(s, axis=0, keepdims=True)
        m = jnp.where(m == NEG_INF, 0.0, m)
        p = jnp.exp2(s - m)
        den = jnp.sum(p, axis=0, keepdims=True)
        p = p * (1.0 / jnp.where(den > 0, den, 1.0))
        p_sum = p if p_sum is None else p_sum + p
        o_cmp.append(jnp.dot(vc, p.astype(BF16), preferred_element_type=F32))

    o_win = []
    win_valid = []
    for c in range(n_wc):
        dist = t_row - (start + c * ck + key_in_chunk)
        win_valid.append(jnp.logical_and(dist >= 0, dist < NSA_WINDOW))
    for r in range(n_rep):
        parts = [jnp.where(win_valid[c], s_win[r][c * ck:(c + 1) * ck], NEG_INF) for c in range(n_wc)]
        offs = [slopes[r] * (c * ck - _chunk_phase(start, c)).astype(F32) for c in range(n_wc)]
        p, _ = _softmax_cols_chunks(parts, offs)
        pv = jnp.dot(vw_band, p.astype(BF16), preferred_element_type=F32)
        den = pv[d:d + 1]
        o_win.append(pv[0:d] / jnp.where(den > 0, den, 1.0))

    imp = _mm_exact_lhs(ovw_ref[...], p_sum, 3)
    blk = lax.broadcasted_iota(jnp.int32, (n_blk, TQ), 0)
    cur = (t0 + lax.broadcasted_iota(jnp.int32, (n_blk, TQ), 1)) // SEL_LEN
    forced = jnp.logical_or(blk == 0, jnp.logical_or(blk == cur, blk == cur - 1))
    score = jnp.where(forced, jnp.inf, jnp.where(blk > cur, NEG_INF, imp))
    sel = jnp.zeros((n_blk, TQ), F32)
    for _ in range(min(N_SEL, n_blk)):
        best = jnp.max(score, axis=0, keepdims=True)
        first = jnp.min(jnp.where(score == best, blk, n_blk), axis=0, keepdims=True)
        hit = blk == first
        sel = jnp.where(hit, 1.0, sel)
        score = jnp.where(hit, NEG_INF, score)
    sel_scr[...] = sel

    ones = jnp.ones((8, TQ), BF16)
    cnt = _mm_nt(ones, sel)
    pool = (lax.broadcasted_iota(jnp.int32, (n_blk, LANES), 0) // bpg
            == lax.broadcasted_iota(jnp.int32, (n_blk, LANES), 1))
    cnt_g = _mm(cnt, jnp.where(pool, 1.0, 0.0))
    for j in range(n_grp + 1):
        list_scr[j] = 0
        valid_scr[j] = 0
    pos = jnp.int32(n_grp)
    for j in reversed(range(n_grp)):
        hit = jnp.logical_and(cnt_g[0, j] > 0.0, j < diag_group)
        pos = pos - hit.astype(jnp.int32)
        slot = jnp.where(hit, pos, n_grp)
        list_scr[slot] = j
        valid_scr[slot] = 1
    n_pairs = (n_grp - pos + 1) // 2
    first = n_grp - 2 * n_pairs

    m_scr[...] = jnp.full(m_scr.shape, NEG_INF, F32)
    acc_scr[...] = jnp.zeros(acc_scr.shape, F32)

    def reduce_group(gidx, live, s_scr, diagonal):
        k0 = pl.multiple_of(gidx * gsz, gsz)
        rows = sel_scr[pl.ds(pl.multiple_of(gidx * bpg, bpg), bpg), :] * live
        v_cat = vs_ref[0, 0, :, pl.ds(k0, gsz)]
        masks = []
        for u in range(bpg):
            picked = rows[u:u + 1, :] > 0.5
            if diagonal:
                picked = jnp.logical_and(picked, k0 + u * SEL_LEN + key_in_blk <= t_row)
            masks.append(picked)
        rel = (k0 - t0).astype(F32)
        for r in range(n_rep):
            parts = [jnp.where(masks[u], s_scr[r, pl.ds(u * SEL_LEN, SEL_LEN), :], NEG_INF) for u in range(bpg)]
            off = slopes[r] * rel
            top = parts[0]
            for part in parts[1:]:
                top = jnp.maximum(top, part)
            m_old = m_scr[r]
            m_new = jnp.maximum(m_old, jnp.max(top, axis=0, keepdims=True) + off)
            m_safe = jnp.where(m_new == NEG_INF, 0.0, m_new)
            alpha = jnp.exp2(m_old - m_safe)
            sub = m_safe - off
            p = jnp.concatenate([jnp.exp2(part - sub) for part in parts], axis=0)
            acc_scr[r] = alpha * acc_scr[r] + jnp.dot(v_cat, p.astype(BF16), preferred_element_type=F32)
            m_scr[r] = m_new

    score_group(list_scr[first], sa_scr)
    reduce_group(diag_group, 1.0, sb_scr, True)

    def body(pi, carry):
        a = first + 2 * pi
        score_group(list_scr[a + 1], sb_scr)
        reduce_group(list_scr[a], valid_scr[a].astype(F32), sa_scr, False)
        score_group(list_scr[a + 2], sa_scr)
        reduce_group(list_scr[a + 1], valid_scr[a + 1].astype(F32), sb_scr, False)
        return carry

    lax.fori_loop(0, n_pairs, body, 0)

    gates = _sigmoid(gl_ref[0, 0])
    outs = []
    for r in range(n_rep):
        acc = acc_scr[r]
        l_fin = acc[d:d + 1]
        o_slc = acc[0:d] / jnp.where(l_fin > 0, l_fin, 1.0)
        g0 = gates[N_BRANCH * r + 0:N_BRANCH * r + 1, :]
        g1 = gates[N_BRANCH * r + 1:N_BRANCH * r + 2, :]
        g2 = gates[N_BRANCH * r + 2:N_BRANCH * r + 3, :]
        outs.append(g0 * o_cmp[r] + g1 * o_slc + g2 * o_win[r])
    o_ref[0] = jnp.concatenate(outs, axis=0).astype(o_ref.dtype)


def _nsa_attention(slopes, q_t, slope_rows, kc, vc_t, ks, vs_t, kw, vw_t, gl_t, ov_w):
    b, g, r, d, t = q_t.shape
    n_blk = t // SEL_LEN
    n_grp = t // SEL_GROUP
    ncp = kc.shape[2]
    ka = kc.shape[3]
    va = vs_t.shape[2]
    idx4 = lambda bi, gi, i: (bi, gi, 0, 0)
    return pl.pallas_call(
        functools.partial(_nsa_kernel, n_rep=r, n_blk=n_blk),
        grid=(b, g, t // TQ),
        in_specs=[pl.BlockSpec(memory_space=pltpu.SMEM),
                  pl.BlockSpec((1, 1, r, d, TQ), lambda bi, gi, i: (bi, gi, 0, 0, i)),
                  pl.BlockSpec((1, r, ka - d, TQ), lambda bi, gi, i: (gi, 0, 0, 0)),
                  pl.BlockSpec((1, 1, ncp, ka), idx4),
                  pl.BlockSpec((1, 1, d, ncp), idx4),
                  pl.BlockSpec((1, 1, t, ka), idx4),
                  pl.BlockSpec((1, 1, va, t), idx4),
                  pl.BlockSpec((1, 1, t, ka), idx4),
                  pl.BlockSpec((1, 1, va, t), idx4),
                  pl.BlockSpec((1, 1, r * N_BRANCH, TQ), lambda bi, gi, i: (bi, gi, 0, i)),
                  pl.BlockSpec(ov_w.shape, lambda bi, gi, i: (0, 0))],
        out_specs=pl.BlockSpec((1, r * d, TQ), lambda bi, gi, i: (bi, gi, i)),
        out_shape=jax.ShapeDtypeStruct((b, g * r * d, t), BF16),
        scratch_shapes=[pltpu.VMEM((n_blk, TQ), F32),
                        pltpu.SMEM((n_grp + 2,), jnp.int32),
                        pltpu.SMEM((n_grp + 2,), jnp.int32),
                        pltpu.VMEM((r, SEL_GROUP, TQ), F32),
                        pltpu.VMEM((r, SEL_GROUP, TQ), F32),
                        pltpu.VMEM((r, 1, TQ), F32),
                        pltpu.VMEM((r, va, TQ), F32)],
        compiler_params=_params("parallel", "parallel", "arbitrary"),
        name="nsa_attention",
    )(slopes, q_t, slope_rows, kc, vc_t, ks, vs_t, kw, vw_t, gl_t, ov_w)


def _alibi_slopes(n_heads):
    return jnp.exp2(-8.0 * jnp.arange(1, n_heads + 1, dtype=F32) / n_heads) * LOG2E


def _truncation_split_bf16(x, n):
    parts = []
    for _ in range(n):
        top = lax.bitcast_convert_type(lax.bitcast_convert_type(x, jnp.uint32) & jnp.uint32(0xFFFF0000), F32)
        parts.append(top.astype(BF16))
        x = x - top
    return parts


def _slope_query_rows(n_heads, n_groups):
    parts = _truncation_split_bf16(_alibi_slopes(n_heads), 3)
    rows = jnp.zeros((n_heads, KEY_AUG), BF16)
    for j, p in enumerate(parts):
        rows = rows.at[:, j].set(p).at[:, 3 + j].set(p)
    rows = jnp.broadcast_to(rows[:, :, None], (n_heads, KEY_AUG, TQ))
    return rows.reshape(n_groups, n_heads // n_groups, KEY_AUG, TQ)


def _augment_keys(k_tm, lo, hi):
    b, g, n, _ = k_tm.shape
    extra = jnp.zeros((n, KEY_AUG), F32)
    extra = extra.at[:, 0:3].set(lo.astype(F32)[:, None]).at[:, 3:6].set(hi.astype(F32)[:, None]).astype(BF16)
    return jnp.concatenate([k_tm, jnp.broadcast_to(extra, (b, g, n, KEY_AUG))], axis=-1)


def _token_key_positions(t):
    pos = jnp.arange(t) % POS_PERIOD
    lo = pos % ATT_CHUNK
    return lo, pos - lo


def _heads_token_major(a, b, t, g):
    return a.reshape(b, t, g, HEAD_DIM).transpose(0, 2, 1, 3)


def _hybrid_layer(x3d, g_pre, g_post, w_in, w_out, sinks, mu, w0, w2, a0, a2, g2, k_k, k_a,
                  r_k, ln_g, ln_b):
    b, t, d = x3d.shape
    gq, rq = SWA_KV_HEADS, SWA_HEADS // SWA_KV_HEADS
    scale = HEAD_DIM ** -0.5 * LOG2E
    wq_t = (w_in[:, :SWA_Q_COLS] * scale).T.astype(BF16)
    no_const = jnp.zeros((SWA_Q_COLS, 1), F32)
    k0 = SWA_Q_COLS + SWA_KV_COLS
    q_fm, v_fm, k_tm, z_rwkv = _norm_proj(
        x3d, g_pre, [(wq_t, no_const), _value_weights(w_in[:, k0:SWA_COLS], gq)],
        w_in[:, SWA_Q_COLS:k0].astype(BF16), w_in[:, SWA_COLS:].astype(BF16))
    lo, hi = _token_key_positions(t)
    q_t = q_fm.reshape(b, gq, rq, HEAD_DIM, t)
    k = _augment_keys(_heads_token_major(k_tm.reshape(b * t, SWA_KV_COLS), b, t, gq), lo, hi)
    v_t = v_fm.reshape(b, gq, HEAD_DIM + VAL_AUG, t)
    o_a = _swa_attention(_alibi_slopes(SWA_HEADS), sinks.astype(F32), q_t, _slope_query_rows(SWA_HEADS, gq), k, v_t)
    o_b = _rwkv_time_mix(z_rwkv, mu, w0, w2, a0, a2, g2, k_k, k_a, r_k.reshape(-1), ln_g, ln_b)
    w_out = w_out.astype(BF16)
    return _out_proj([o_a, o_b], [True, False], [w_out[:SWA_Q_COLS], w_out[SWA_Q_COLS:]], x3d, g_post)


def _overlap_weights(n_blk, n_cmp_pad, n_cmp):
    ratio, span = SEL_LEN // CMP_STRIDE, CMP_LEN // CMP_STRIDE
    w = np.zeros((n_blk, n_cmp_pad), np.float32)
    for j in range(n_blk):
        for a in range(ratio):
            for s in range(span):
                c = ratio * j + a - s
                if 0 <= c < n_cmp:
                    w[j, c] += 1.0
    return jnp.asarray(w, BF16)


def _nsa_layer(x3d, g_pre, g_post, w_in, w_out, pos_k, w1_k, w2_k, pos_v, w1_v, w2_v):
    b, t, d = x3d.shape
    g, r = NSA_KV_HEADS, NSA_HEADS // NSA_KV_HEADS
    off = NSA_Q_COLS
    kv = NSA_KV_COLS
    scale = HEAD_DIM ** -0.5 * LOG2E
    wq_t = (w_in[:, :off] * scale).T.astype(BF16)
    w_cols = lambda j: w_in[:, off + j * kv:off + (j + 1) * kv]
    w_f = jnp.concatenate([w_cols(0), w_cols(1), w_in[:, off + 6 * kv:]], axis=1)
    w_f = jnp.pad(w_f, ((0, 0), (0, (-w_f.shape[1]) % LANES))).astype(BF16)
    w_b = jnp.concatenate([w_cols(2), w_cols(4)], axis=1).astype(BF16)
    q_fm, vs_fm, vw_fm, z_b, z_f = _norm_proj(
        x3d, g_pre, [(wq_t, jnp.zeros((off, 1), F32)), _value_weights(w_cols(3), g), _value_weights(w_cols(5), g)],
        w_b, w_f)
    z_b = z_b.reshape(b * t, 2 * kv)
    z_f = z_f.reshape(b * t, -1)
    cols = lambda j: z_f[:, j * kv:(j + 1) * kv]
    q_t = q_fm.reshape(b, g, r, HEAD_DIM, t)
    n_half = t // CMP_STRIDE
    n_cmp = (t - CMP_LEN) // CMP_STRIDE + 1
    half_rows = lambda a: a.reshape(b, n_half, CMP_STRIDE, g, HEAD_DIM).transpose(0, 3, 1, 2, 4).reshape(
        b, g, n_half, CMP_STRIDE * HEAD_DIM)
    flat_pos = lambda p: p.reshape(CMP_LEN // CMP_STRIDE, CMP_STRIDE * HEAD_DIM)
    flat_w1 = lambda w: w.reshape(CMP_LEN * HEAD_DIM, CMP_HIDDEN).astype(BF16)
    kc, vc_t = _compress(half_rows(cols(0)), half_rows(cols(1)), flat_pos(pos_k), flat_pos(pos_v),
                         flat_w1(w1_k), flat_w1(w1_v), w2_k.astype(BF16), w2_v.astype(BF16), n_cmp)
    lo, hi = _token_key_positions(t)
    cmp_pos = jnp.arange(n_half) * CMP_STRIDE
    kc = _augment_keys(kc, cmp_pos, jnp.zeros_like(cmp_pos))
    ks = _augment_keys(_heads_token_major(z_b[:, :kv], b, t, g), lo, hi)
    vs_t = vs_fm.reshape(b, g, HEAD_DIM + VAL_AUG, t)
    kw = _augment_keys(_heads_token_major(z_b[:, kv:], b, t, g), lo, hi)
    vw_t = vw_fm.reshape(b, g, HEAD_DIM + VAL_AUG, t)
    gl_t = z_f[:, 2 * kv:2 * kv + NSA_HEADS * N_BRANCH].reshape(b, t, g, r * N_BRANCH).transpose(0, 2, 3, 1)
    ov_w = _overlap_weights(t // SEL_LEN, n_half, n_cmp)
    o = _nsa_attention(_alibi_slopes(NSA_HEADS), q_t, _slope_query_rows(NSA_HEADS, g), kc, vc_t, ks, vs_t,
                       kw, vw_t, gl_t, ov_w)
    return _out_proj([o], [True], [w_out.astype(BF16)], x3d, g_post)


def kernel(x, mix_pre_g, mix_post_g, ffn_pre_g, ffn_post_g, hy_w_in, hy_w_out, swa_sinks, rwkv_mu, rwkv_w0, rwkv_w2, rwkv_a0, rwkv_a2, rwkv_g2, rwkv_k_k, rwkv_k_a, rwkv_r_k, rwkv_ln_g, rwkv_ln_b, nsa_w_in, nsa_w_out, nsa_cmp_pos_k, nsa_cmp_w1_k, nsa_cmp_w2_k, nsa_cmp_pos_v, nsa_cmp_w1_v, nsa_cmp_w2_v, ffn_w_up, ffn_conv_w, ffn_conv_b, ffn_w_down):
    b, t, d = x.shape
    depth = mix_pre_g.shape[0]
    for layer in range(depth):
        i = layer // 2
        if layer % 2 == 0:
            x = _hybrid_layer(x, mix_pre_g[layer], mix_post_g[layer], hy_w_in[i], hy_w_out[i],
                              swa_sinks[i], rwkv_mu[i], rwkv_w0[i], rwkv_w2[i], rwkv_a0[i], rwkv_a2[i],
                              rwkv_g2[i], rwkv_k_k[i], rwkv_k_a[i], rwkv_r_k[i], rwkv_ln_g[i], rwkv_ln_b[i])
        else:
            x = _nsa_layer(x, mix_pre_g[layer], mix_post_g[layer], nsa_w_in[i], nsa_w_out[i],
                           nsa_cmp_pos_k[i], nsa_cmp_w1_k[i], nsa_cmp_w2_k[i],
                           nsa_cmp_pos_v[i], nsa_cmp_w1_v[i], nsa_cmp_w2_v[i])
        x = _conv_ffn(x.reshape(b * t, d), t, ffn_pre_g[layer], ffn_w_up[layer].astype(BF16), ffn_conv_w[layer],
                      ffn_conv_b[layer], ffn_w_down[layer].astype(BF16), ffn_post_g[layer]).reshape(b, t, d)
    return x
```

```python
import functools
import math

import numpy as np
import jax
import jax.numpy as jnp
from jax import lax
from jax.experimental import pallas as pl
from jax.experimental.pallas import tpu as pltpu

F32 = jnp.float32
BF16 = jnp.bfloat16

D_MODEL = 1024
HEAD_DIM = 64
SWA_HEADS = 8
SWA_KV_HEADS = 2
SWA_WINDOW = 128
RWKV_HEADS = 8
RWKV_DIM = RWKV_HEADS * HEAD_DIM
RWKV_W_LORA = 64
RWKV_A_LORA = 64
RWKV_G_LORA = 128
RWKV_GN_EPS = 64e-5
NSA_HEADS = 16
NSA_KV_HEADS = 4
CMP_LEN = 32
CMP_STRIDE = 16
CMP_HIDDEN = 256
SEL_LEN = 64
N_SEL = 8
NSA_WINDOW = 256
N_BRANCH = 3
D_FF = 2816
CONV_WIDTH = 3
NORM_EPS = 1e-6

SWA_Q_COLS = SWA_HEADS * HEAD_DIM
SWA_KV_COLS = SWA_KV_HEADS * HEAD_DIM
SWA_COLS = SWA_Q_COLS + 2 * SWA_KV_COLS
NSA_KV_COLS = NSA_KV_HEADS * HEAD_DIM
NSA_Q_COLS = NSA_HEADS * HEAD_DIM

LANES = 128
VMEM_LIMIT = 56 * 1024 * 1024

TQ = 256
ATT_CHUNK = 128
POS_PERIOD = 512
KEY_AUG = 16
VAL_AUG = 16
SEL_GROUP = POS_PERIOD
RWKV_CHUNK = 64
RWKV_QUAD = 4
RWKV_BATCH = 8
NEG_INF = float("-inf")
LOG2E = math.log2(math.e)


def _params(*sem):
    return pltpu.CompilerParams(dimension_semantics=sem, vmem_limit_bytes=VMEM_LIMIT)


def _rms(x, g):
    return x * lax.rsqrt(jnp.mean(x * x, axis=-1, keepdims=True) + NORM_EPS) * g


def _mm(a, b):
    return jnp.dot(a.astype(BF16), b.astype(BF16), preferred_element_type=F32)


def _mm_nt(a, b):
    return lax.dot_general(a.astype(BF16), b.astype(BF16), (((1,), (1,)), ((), ())),
                           preferred_element_type=F32)


def _mm_tn(a, b):
    return lax.dot_general(a.astype(BF16), b.astype(BF16), (((0,), (0,)), ((), ())),
                           preferred_element_type=F32)


def _split_bf16(x, n):
    parts = []
    for _ in range(n):
        p = x.astype(BF16)
        parts.append(p)
        x = x - p.astype(F32)
    return parts


def _mm_exact_lhs(a_bf16, x, n):
    out = None
    for p in _split_bf16(x, n):
        t = jnp.dot(a_bf16, p, preferred_element_type=F32)
        out = t if out is None else out + t
    return out


def _mm_exact_rhs(x, b_bf16, n):
    out = None
    for p in _split_bf16(x, n):
        t = jnp.dot(p, b_bf16, preferred_element_type=F32)
        out = t if out is None else out + t
    return out


def _sigmoid(x):
    return 1.0 / (1.0 + jnp.exp(-x))


def _gelu_tanh(x):
    c = math.sqrt(2.0 / math.pi)
    return 0.5 * x * (1.0 + jnp.tanh(c * (x + 0.044715 * (x * x * x))))


def _norm_proj_kernel(*refs, n_fm):
    x_ref, g_ref = refs[0:2]
    wfm_refs = refs[2:2 + 2 * n_fm]
    wb_ref, wf_ref = refs[2 + 2 * n_fm:4 + 2 * n_fm]
    out_refs = refs[4 + 2 * n_fm:]
    h = _rms(x_ref[0], g_ref[...]).astype(BF16)
    for j in range(n_fm):
        w_ref, c_ref, o_ref = wfm_refs[2 * j], wfm_refs[2 * j + 1], out_refs[j]
        o_ref[0] = (lax.dot_general(w_ref[...], h, (((1,), (1,)), ((), ())), preferred_element_type=F32)
                    + c_ref[...]).astype(o_ref.dtype)
    out_refs[n_fm][0] = jnp.dot(h, wb_ref[...], preferred_element_type=F32).astype(out_refs[n_fm].dtype)
    out_refs[n_fm + 1][0] = jnp.dot(h, wf_ref[...], preferred_element_type=F32)


def _norm_proj(x3d, g, fm_weights, w_b, w_f, tm=512):
    b, t, d = x3d.shape
    cb, cf = w_b.shape[1], w_f.shape[1]
    const = lambda bi, j: (0, 0)
    fm_specs, fm_args, fm_outs, fm_shapes = [], [], [], []
    for w_t, c in fm_weights:
        f = w_t.shape[0]
        fm_specs += [pl.BlockSpec((f, d), const), pl.BlockSpec((f, 1), const)]
        fm_args += [w_t, c]
        fm_outs.append(pl.BlockSpec((1, f, tm), lambda bi, j: (bi, 0, j)))
        fm_shapes.append(jax.ShapeDtypeStruct((b, f, t), BF16))
    return pl.pallas_call(
        functools.partial(_norm_proj_kernel, n_fm=len(fm_weights)),
        grid=(b, t // tm),
        in_specs=[pl.BlockSpec((1, tm, d), lambda bi, j: (bi, j, 0)), pl.BlockSpec((1, d), const)]
        + fm_specs + [pl.BlockSpec((d, cb), const), pl.BlockSpec((d, cf), const)],
        out_specs=fm_outs + [pl.BlockSpec((1, tm, cb), lambda bi, j: (bi, j, 0)),
                             pl.BlockSpec((1, tm, cf), lambda bi, j: (bi, j, 0))],
        out_shape=fm_shapes + [jax.ShapeDtypeStruct((b, t, cb), BF16),
                               jax.ShapeDtypeStruct((b, t, cf), F32)],
        compiler_params=_params("parallel", "parallel"),
        name="norm_proj",
    )(x3d, g.reshape(1, d), *fm_args, w_b, w_f)


def _value_weights(w_v, n_groups):
    d = w_v.shape[0]
    w_t = w_v.T.reshape(n_groups, HEAD_DIM, d)
    w_t = jnp.concatenate([w_t, jnp.zeros((n_groups, VAL_AUG, d), w_t.dtype)], axis=1)
    ones_row = jnp.zeros((n_groups, HEAD_DIM + VAL_AUG, 1), F32).at[:, HEAD_DIM].set(1.0)
    return (w_t.reshape(n_groups * (HEAD_DIM + VAL_AUG), d).astype(BF16),
            ones_row.reshape(n_groups * (HEAD_DIM + VAL_AUG), 1))


def _out_proj_kernel(*refs, feature_major):
    n_parts = len(feature_major)
    o_refs = refs[:n_parts]
    w_refs = refs[n_parts:2 * n_parts]
    x_ref, g_ref, y_ref = refs[2 * n_parts:]
    m = None
    for o_ref, w_ref, fm in zip(o_refs, w_refs, feature_major):
        if fm:
            t = lax.dot_general(o_ref[0].astype(BF16), w_ref[...], (((0,), (0,)), ((), ())),
                                preferred_element_type=F32)
        else:
            t = jnp.dot(o_ref[0].astype(BF16), w_ref[...], preferred_element_type=F32)
        m = t if m is None else m + t
    y_ref[0] = x_ref[0] + _rms(m, g_ref[...])


def _out_proj(parts, feature_major, weights, x3d, g, tm=512):
    b, t, d = x3d.shape
    in_specs = []
    for p, fm in zip(parts, feature_major):
        if fm:
            in_specs.append(pl.BlockSpec((1, p.shape[1], tm), lambda bi, j: (bi, 0, j)))
        else:
            in_specs.append(pl.BlockSpec((1, tm, p.shape[2]), lambda bi, j: (bi, j, 0)))
    in_specs += [pl.BlockSpec(w.shape, lambda bi, j: (0, 0)) for w in weights]
    in_specs += [pl.BlockSpec((1, tm, d), lambda bi, j: (bi, j, 0)), pl.BlockSpec((1, d), lambda bi, j: (0, 0))]
    return pl.pallas_call(
        functools.partial(_out_proj_kernel, feature_major=tuple(feature_major)),
        grid=(b, t // tm),
        in_specs=in_specs,
        out_specs=pl.BlockSpec((1, tm, d), lambda bi, j: (bi, j, 0)),
        out_shape=jax.ShapeDtypeStruct((b, t, d), F32),
        compiler_params=_params("parallel", "parallel"),
        name="out_proj",
    )(*parts, *weights, x3d, g.reshape(1, d))


FFN_HALO = 16


def _ffn_kernel(xh_ref, x_ref, gpre_ref, wg_ref, wv_ref, cw_ref, cb_ref, wd_ref, gpost_ref,
                y_ref, h_scr, acc_scr, *, tm, tiles_per_seq):
    i = pl.program_id(0)
    j = pl.program_id(1)
    nj = pl.num_programs(1)

    @pl.when(j == 0)
    def _():
        xe = jnp.concatenate([xh_ref[...], x_ref[...]], axis=0)
        h_scr[...] = _rms(xe, gpre_ref[...]).astype(BF16)

    h = h_scr[...]
    gate = jnp.dot(h, wg_ref[...], preferred_element_type=F32)
    val = jnp.dot(h[FFN_HALO:], wv_ref[...], preferred_element_type=F32)
    row = lax.broadcasted_iota(jnp.int32, gate.shape, 0)
    seq_start = (i % tiles_per_seq) == 0
    gate = jnp.where(jnp.logical_and(row < FFN_HALO, seq_start), 0.0, gate)
    cw = cw_ref[...]
    conv = cb_ref[...] + gate[FFN_HALO:] * cw[2:3]
    conv = conv + gate[FFN_HALO - 2:-2] * cw[0:1]
    conv = conv + gate[FFN_HALO - 1:-1] * cw[1:2]
    act = _gelu_tanh(conv) * val
    part = jnp.dot(act.astype(BF16), wd_ref[...], preferred_element_type=F32)

    @pl.when(j == 0)
    def _():
        acc_scr[...] = part

    @pl.when(j > 0)
    def _():
        acc_scr[...] += part

    @pl.when(j == nj - 1)
    def _():
        y_ref[...] = x_ref[...] + _rms(acc_scr[...], gpost_ref[...])


def _conv_ffn(x2d, seq_len, g_pre, w_up, conv_w, conv_b, w_down, g_post, tm=512, nf=2):
    n, d = x2d.shape
    tf = D_FF // nf
    hb = tm // FFN_HALO
    return pl.pallas_call(
        functools.partial(_ffn_kernel, tm=tm, tiles_per_seq=seq_len // tm),
        grid=(n // tm, nf),
        in_specs=[pl.BlockSpec((FFN_HALO, d), lambda i, j: (jnp.maximum(i * hb - 1, 0), 0)),
                  pl.BlockSpec((tm, d), lambda i, j: (i, 0)),
                  pl.BlockSpec((1, d), lambda i, j: (0, 0)),
                  pl.BlockSpec((d, tf), lambda i, j: (0, j)),
                  pl.BlockSpec((d, tf), lambda i, j: (0, nf + j)),
                  pl.BlockSpec((CONV_WIDTH, tf), lambda i, j: (0, j)),
                  pl.BlockSpec((1, tf), lambda i, j: (0, j)),
                  pl.BlockSpec((tf, d), lambda i, j: (j, 0)),
                  pl.BlockSpec((1, d), lambda i, j: (0, 0))],
        out_specs=pl.BlockSpec((tm, d), lambda i, j: (i, 0)),
        out_shape=jax.ShapeDtypeStruct((n, d), F32),
        scratch_shapes=[pltpu.VMEM((tm + FFN_HALO, d), BF16), pltpu.VMEM((tm, d), F32)],
        compiler_params=_params("parallel", "arbitrary"),
        name="conv_ffn",
    )(x2d, x2d, g_pre.reshape(1, d), w_up, w_up, conv_w, conv_b.reshape(1, D_FF), w_down,
      g_post.reshape(1, d))


def _chunk_phase(start, c):
    return ((start // ATT_CHUNK + c) % (POS_PERIOD // ATT_CHUNK)) * ATT_CHUNK


def _softmax_cols_chunks(parts, offs, extra=None):
    m = extra
    for part, off in zip(parts, offs):
        cm = jnp.max(part, axis=0, keepdims=True) + off
        m = cm if m is None else jnp.maximum(m, cm)
    m = jnp.where(m == NEG_INF, 0.0, m)
    p = jnp.concatenate([jnp.exp2(part - (m - off)) for part, off in zip(parts, offs)], axis=0)
    return p, m


def _swa_kernel(slope_ref, sink_ref, q_ref, s3_ref, k_ref, v_ref, o_ref, *, n_rep, window):
    g = pl.program_id(1)
    i = pl.program_id(2)
    t0 = i * TQ
    d = HEAD_DIM
    ck = ATT_CHUNK
    n_c = (window + TQ) // ck
    start = pl.multiple_of(jnp.maximum(t0 - window, 0), LANES)
    kb = k_ref[0, 0, pl.ds(start, n_c * ck), :]
    vb = v_ref[0, 0, :, pl.ds(start, n_c * ck)]
    t_row = t0 + lax.broadcasted_iota(jnp.int32, (1, TQ), 1)
    key_in_chunk = lax.broadcasted_iota(jnp.int32, (ck, TQ), 0)
    valid = []
    for c in range(n_c):
        dist = t_row - (start + c * ck + key_in_chunk)
        valid.append(jnp.logical_and(dist >= 0, dist < window))
    rel_t = (t_row - start).astype(F32)
    scores = []
    for r in range(n_rep):
        q_aug = jnp.concatenate([q_ref[0, 0, r], s3_ref[0, r]], axis=0)
        scores.append(jnp.dot(kb, q_aug, preferred_element_type=F32))
    probs, tails = [], []
    for r in range(n_rep):
        slope = slope_ref[g * n_rep + r]
        parts = [jnp.where(valid[c], scores[r][c * ck:(c + 1) * ck], NEG_INF) for c in range(n_c)]
        offs = [slope * (c * ck - _chunk_phase(start, c)).astype(F32) for c in range(n_c)]
        sink = sink_ref[g * n_rep + r] * LOG2E + slope * rel_t
        p, m = _softmax_cols_chunks(parts, offs, extra=sink)
        probs.append(p.astype(BF16))
        tails.append(jnp.exp2(sink - m))
    outs = []
    for r in range(n_rep):
        pv = jnp.dot(vb, probs[r], preferred_element_type=F32)
        outs.append(pv[0:d] / (pv[d:d + 1] + tails[r]))
    o_ref[0] = jnp.concatenate(outs, axis=0).astype(o_ref.dtype)


def _swa_attention(slopes, sinks, q_t, slope_rows, k, v_t):
    b, g, r, d, t = q_t.shape
    ka, va = k.shape[3], v_t.shape[2]
    smem = pl.BlockSpec(memory_space=pltpu.SMEM)
    return pl.pallas_call(
        functools.partial(_swa_kernel, n_rep=r, window=SWA_WINDOW),
        grid=(b, g, t // TQ),
        in_specs=[smem, smem,
                  pl.BlockSpec((1, 1, r, d, TQ), lambda bi, gi, i: (bi, gi, 0, 0, i)),
                  pl.BlockSpec((1, r, ka - d, TQ), lambda bi, gi, i: (gi, 0, 0, 0)),
                  pl.BlockSpec((1, 1, t, ka), lambda bi, gi, i: (bi, gi, 0, 0)),
                  pl.BlockSpec((1, 1, va, t), lambda bi, gi, i: (bi, gi, 0, 0))],
        out_specs=pl.BlockSpec((1, r * d, TQ), lambda bi, gi, i: (bi, gi, i)),
        out_shape=jax.ShapeDtypeStruct((b, g * r * d, t), BF16),
        compiler_params=_params("parallel", "parallel", "parallel"),
        name="swa_attention",
    )(slopes, sinks, q_t, slope_rows, k, v_t)


def _rwkv_kernel(zr_ref, zk_ref, zv_ref, zl_ref, mur_ref, muk_ref, muv_ref, mul_ref,
                 w0_ref, a0_ref, kk_ref, ka_ref, rk_ref, lng_ref, lnb_ref,
                 w2_ref, a2_ref, g2_ref, o_ref, prev_scr, s_scr, *, nb):
    c = pl.program_id(2)
    cs = RWKV_CHUNK
    wd = RWKV_QUAD * HEAD_DIM
    each = lambda f, *cols: [f(*args) for args in zip(*cols)]

    @pl.when(c == 0)
    def _():
        prev_scr[...] = jnp.zeros_like(prev_scr)
        s_scr[...] = jnp.zeros_like(s_scr)

    row = lax.broadcasted_iota(jnp.int32, (cs, wd), 0)

    def shifted(z_ref, mu_ref, idx):
        outs = []
        for bb in range(nb):
            z = z_ref[bb]
            prev = prev_scr[idx * nb + bb, 0:1, :]
            zp = jnp.where(row == 0, prev, pltpu.roll(z, 1, 0))
            prev_scr[idx * nb + bb, 0:1, :] = z[cs - 1:cs, :]
            outs.append(z + (zp - z) * mu_ref[...])
        return outs

    r = shifted(zr_ref, mur_ref, 0)
    k = shifted(zk_ref, muk_ref, 1)
    v = shifted(zv_ref, muv_ref, 2)
    lo = shifted(zl_ref, mul_ref, 3)

    yw = each(lambda x: w0_ref[...] + _mm(jnp.tanh(x), w2_ref[...]), lo)
    ya = each(lambda x: a0_ref[...] + _mm(x, a2_ref[...]), lo)
    gate = each(lambda x: _mm(_sigmoid(x), g2_ref[...]), lo)
    softplus = each(lambda x: jnp.maximum(-x, 0.0) + jnp.log(1.0 + jnp.exp(-jnp.abs(x))), yw)
    logd = each(lambda x: -jnp.exp(-x - 0.5), softplus)
    a = each(_sigmoid, ya)

    i0 = lax.broadcasted_iota(jnp.int32, (wd, wd), 0)
    i1 = lax.broadcasted_iota(jnp.int32, (wd, wd), 1)
    same_head = (i0 // HEAD_DIM) == (i1 // HEAD_DIM)
    ones_head = jnp.where(same_head, 1.0, 0.0).astype(BF16)

    def head_sum(x):
        return _mm_exact_rhs(x, ones_head, 2)

    kk = each(lambda x: x * kk_ref[...], k)
    kk_ss = each(lambda x: head_sum(x * x), kk)
    kk = each(lambda x, ss: x / jnp.maximum(jnp.sqrt(ss), 1e-12), kk, kk_ss)
    kmod = each(lambda x, y: x * (1.0 + (y - 1.0) * ka_ref[...]), k, a)
    b = each(lambda x, y: x * y, kk, a)

    c0 = lax.broadcasted_iota(jnp.int32, (cs, cs), 0)
    c1 = lax.broadcasted_iota(jnp.int32, (cs, cs), 1)
    tri = jnp.where(c0 >= c1, 1.0, 0.0).astype(BF16)
    cl = each(lambda x: _mm_exact_lhs(tri, x, 3), logd)
    cl_end = each(lambda x: x[cs - 1:cs, :], cl)
    w_inc = each(jnp.exp, cl)
    w_exc = each(lambda x, y: jnp.exp(x - y), cl, logd)
    w_inv = each(lambda x: jnp.exp(-x), cl)
    w_rem = each(lambda x, y: jnp.exp(y - x), cl, cl_end)

    def stack(x):
        return jnp.where(same_head, jnp.concatenate([x] * RWKV_QUAD, axis=0), 0.0)

    mul_stack = lambda x, y: stack(x * y)
    x_kk = each(mul_stack, kk, w_exc)
    x_r = each(mul_stack, r, w_inc)
    x_b = each(mul_stack, b, w_inv)
    x_k = each(mul_stack, kmod, w_inv)
    x_v = each(stack, v)
    x_bd = each(mul_stack, b, w_rem)
    x_kd = each(mul_stack, kmod, w_rem)

    strict = (i0 % cs) > (i1 % cs)
    incl = (i0 % cs) >= (i1 % cs)
    a_ub = each(lambda x, y: jnp.where(strict, _mm_nt(x, y), 0.0), x_kk, x_b)
    a_uk = each(lambda x, y: jnp.where(strict, _mm_nt(x, y), 0.0), x_kk, x_k)
    a_rb = each(lambda x, y: jnp.where(incl, _mm_nt(x, y), 0.0), x_r, x_b)
    a_rk = each(lambda x, y: jnp.where(incl, _mm_nt(x, y), 0.0), x_r, x_k)

    eye = i0 == i1
    pw = a_ub
    t_inv = each(lambda x: jnp.where(eye, 1.0, 0.0) - x, a_ub)
    for _ in range(int(math.log2(cs)) - 1):
        pw = each(lambda x: _mm(x, x), pw)
        t_inv = each(lambda t, x: t + _mm(t, x), t_inv, pw)

    p1 = each(_mm, t_inv, x_kk)
    av = each(_mm, a_uk, x_v)
    p2 = each(_mm, t_inv, av)
    q1 = each(lambda x, y, z: x - _mm(y, z), x_r, a_rb, p1)
    q2 = each(lambda w, x, y, z: _mm(w, x) - _mm(y, z), a_rk, x_v, a_rb, p2)
    g_mat = each(lambda e, x, y: jnp.where(eye, jnp.exp(e), 0.0) - _mm_tn(x, y), cl_end, x_bd, p1)
    h_mat = each(lambda w, x, y, z: _mm_tn(w, x) - _mm_tn(y, z), x_kd, x_v, x_bd, p2)

    s0 = [s_scr[bb] for bb in range(nb)]
    ys = each(lambda x, s, y: _mm(x, s) + y, q1, s0, q2)
    s1 = each(lambda x, s, y: _mm(x, s) + y, g_mat, s0, h_mat)
    for bb in range(nb):
        s_scr[bb] = s1[bb]

    def fold(x):
        y = x[0:cs]
        for hd in range(1, RWKV_QUAD):
            y = y + x[hd * cs:(hd + 1) * cs]
        return y

    y = each(fold, ys)

    inv_n = 1.0 / HEAD_DIM
    mean = each(lambda x: head_sum(x) * inv_n, y)
    yc = each(lambda x, m: x - m, y, mean)
    var = each(lambda x: head_sum(x * x) * inv_n, yc)
    yn = each(lambda x, s: x * lax.rsqrt(s + RWKV_GN_EPS) * lng_ref[...] + lnb_ref[...], yc, var)
    bonus = each(lambda x, y, z: head_sum(x * y * rk_ref[...]) * z, r, kmod, v)
    for bb in range(nb):
        o_ref[bb] = (yn[bb] + bonus[bb]) * gate[bb]


def _rwkv_time_mix(z3, mu, w0, w2, a0, a2, g2, k_k, k_a, r_k, ln_g, ln_b):
    b, t, _ = z3.shape
    wd = RWKV_QUAD * HEAD_DIM
    nq = RWKV_DIM // wd
    base = 0
    lora_rows = RWKV_W_LORA + RWKV_A_LORA + RWKV_G_LORA
    assert lora_rows == wd

    def pad_rows(w, start):
        return jnp.zeros((wd, RWKV_DIM), F32).at[start:start + w.shape[0]].set(w).astype(BF16)

    w2p = pad_rows(w2, 0)
    a2p = pad_rows(a2, RWKV_W_LORA)
    g2p = pad_rows(g2, RWKV_W_LORA + RWKV_A_LORA)
    mu2 = mu.reshape(1, -1)
    row = lambda x: x.reshape(1, RWKV_DIM)
    nb = RWKV_BATCH if b % RWKV_BATCH == 0 else 1
    zspec = lambda off: pl.BlockSpec((nb, RWKV_CHUNK, wd), lambda bi, qi, ci: (bi, ci, base + off + qi))
    zlspec = pl.BlockSpec((nb, RWKV_CHUNK, wd), lambda bi, qi, ci: (bi, ci, base + 3 * nq))
    muspec = lambda off: pl.BlockSpec((1, wd), lambda bi, qi, ci: (0, off + qi))
    mulspec = pl.BlockSpec((1, wd), lambda bi, qi, ci: (0, 3 * nq))
    pspec = pl.BlockSpec((1, wd), lambda bi, qi, ci: (0, qi))
    wspec = pl.BlockSpec((wd, wd), lambda bi, qi, ci: (0, qi))
    return pl.pallas_call(
        functools.partial(_rwkv_kernel, nb=nb),
        grid=(b // nb, nq, t // RWKV_CHUNK),
        in_specs=[zspec(0), zspec(nq), zspec(2 * nq), zlspec,
                  muspec(0), muspec(nq), muspec(2 * nq), mulspec,
                  pspec, pspec, pspec, pspec, pspec, pspec, pspec,
                  wspec, wspec, wspec],
        out_specs=pl.BlockSpec((nb, RWKV_CHUNK, wd), lambda bi, qi, ci: (bi, ci, qi)),
        out_shape=jax.ShapeDtypeStruct((b, t, RWKV_DIM), F32),
        scratch_shapes=[pltpu.VMEM((4 * nb, 8, wd), F32), pltpu.VMEM((nb, wd, wd), F32)],
        compiler_params=_params("parallel", "parallel", "arbitrary"),
        name="rwkv7_time_mix",
    )(z3, z3, z3, z3, mu2, mu2, mu2, mu2,
      row(w0), row(a0), row(k_k), row(k_a), row(r_k), row(ln_g), row(ln_b),
      w2p, a2p, g2p)


def _compress_kernel(kb_ref, vb_ref, posk_ref, posv_ref, w1k_ref, w1v_ref, w2k_ref, w2v_ref,
                     kc_ref, vc_ref, *, n_cmp):
    half = CMP_STRIDE * HEAD_DIM

    def run(blk_ref, pos_ref, w1_ref, w2_ref):
        a = blk_ref[0, 0]
        lo = _mm(a + pos_ref[0:1, :], w1_ref[0:half, :])
        hi = _mm(a + pos_ref[1:2, :], w1_ref[half:2 * half, :])
        n_half = a.shape[0]
        hid = lo + pltpu.roll(hi, n_half - 1, 0)
        out = _mm(_gelu_tanh(hid), w2_ref[...])
        rows = lax.broadcasted_iota(jnp.int32, out.shape, 0)
        return jnp.where(rows < n_cmp, out, 0.0)

    kc_ref[0, 0] = run(kb_ref, posk_ref, w1k_ref, w2k_ref).astype(BF16)
    vc_ref[0, 0] = run(vb_ref, posv_ref, w1v_ref, w2v_ref).T.astype(BF16)


def _compress(kc_blocks, vc_blocks, pos_k, pos_v, w1_k, w1_v, w2_k, w2_v, n_cmp):
    b, g, nh, hw = kc_blocks.shape
    blk = pl.BlockSpec((1, 1, nh, hw), lambda bi, gi: (bi, gi, 0, 0))
    full = lambda a: pl.BlockSpec(a.shape, lambda bi, gi: (0,) * a.ndim)
    return pl.pallas_call(
        functools.partial(_compress_kernel, n_cmp=n_cmp),
        grid=(b, g),
        in_specs=[blk, blk, full(pos_k), full(pos_v), full(w1_k), full(w1_v), full(w2_k), full(w2_v)],
        out_specs=[pl.BlockSpec((1, 1, nh, HEAD_DIM), lambda bi, gi: (bi, gi, 0, 0)),
                   pl.BlockSpec((1, 1, HEAD_DIM, nh), lambda bi, gi: (bi, gi, 0, 0))],
        out_shape=[jax.ShapeDtypeStruct((b, g, nh, HEAD_DIM), BF16),
                   jax.ShapeDtypeStruct((b, g, HEAD_DIM, nh), BF16)],
        compiler_params=_params("parallel", "parallel"),
        name="nsa_compress",
    )(kc_blocks, vc_blocks, pos_k, pos_v, w1_k, w1_v, w2_k, w2_v)


def _nsa_kernel(slope_ref, q_ref, s3_ref, kc_ref, vc_ref, ks_ref, vs_ref, kw_ref, vw_ref, gl_ref, ovw_ref,
                o_ref, sel_scr, list_scr, valid_scr, sa_scr, sb_scr, m_scr, acc_scr, *, n_rep, n_blk):
    g = pl.program_id(1)
    i = pl.program_id(2)
    t0 = i * TQ
    d = HEAD_DIM
    ck = ATT_CHUNK
    gsz = SEL_GROUP
    bpg = gsz // SEL_LEN
    n_grp = n_blk // bpg
    q_aug = [jnp.concatenate([q_ref[0, 0, r], s3_ref[0, r]], axis=0) for r in range(n_rep)]
    slopes = [slope_ref[g * n_rep + r] for r in range(n_rep)]
    t_row = t0 + lax.broadcasted_iota(jnp.int32, (1, TQ), 1)
    key_in_chunk = lax.broadcasted_iota(jnp.int32, (ck, TQ), 0)
    key_in_blk = lax.broadcasted_iota(jnp.int32, (SEL_LEN, TQ), 0)

    kc = kc_ref[0, 0]
    vc = vc_ref[0, 0]
    s_cmp = [jnp.dot(kc, q_aug[r], preferred_element_type=F32) for r in range(n_rep)]
    n_wc = (NSA_WINDOW + TQ) // ck
    start = pl.multiple_of(jnp.maximum(t0 - NSA_WINDOW, 0), LANES)
    kw_band = kw_ref[0, 0, pl.ds(start, n_wc * ck), :]
    vw_band = vw_ref[0, 0, :, pl.ds(start, n_wc * ck)]
    s_win = [jnp.dot(kw_band, q_aug[r], preferred_element_type=F32) for r in range(n_rep)]

    def score_group(gidx, s_scr):
        k_cat = ks_ref[0, 0, pl.ds(pl.multiple_of(gidx * gsz, gsz), gsz), :]
        for r in range(n_rep):
            s_scr[r] = jnp.dot(k_cat, q_aug[r], preferred_element_type=F32)

    diag_group = (t0 + TQ - 1) // gsz
    score_group(diag_group, sb_scr)

    cmp_end = lax.broadcasted_iota(jnp.int32, (kc.shape[0], TQ), 0) * CMP_STRIDE + (CMP_LEN - 1)
    valid_c = cmp_end <= t_row
    p_sum = None
    o_cmp = []
    for r in range(n_rep):
        s = jnp.where(valid_c, s_cmp[r], NEG_INF)
        m = jnp.max(s, axis=0, keepdims=True)
        m = jnp.where(m == NEG_INF, 0.0, m)
        p = jnp.exp2(s - m)
        den = jnp.sum(p, axis=0, keepdims=True)
        p = p * (1.0 / jnp.where(den > 0, den, 1.0))
        p_sum = p if p_sum is None else p_sum + p
        o_cmp.append(jnp.dot(vc, p.astype(BF16), preferred_element_type=F32))

    imp = _mm_exact_lhs(ovw_ref[...], p_sum, 3)
    blk = lax.broadcasted_iota(jnp.int32, (n_blk, TQ), 0)
    cur = (t0 + lax.broadcasted_iota(jnp.int32, (n_blk, TQ), 1)) // SEL_LEN
    forced = jnp.logical_or(blk == 0, jnp.logical_or(blk == cur, blk == cur - 1))
    score = jnp.where(forced, jnp.inf, jnp.where(blk > cur, NEG_INF, imp))
    sel = jnp.zeros((n_blk, TQ), F32)
    for _ in range(min(N_SEL, n_blk)):
        best = jnp.max(score, axis=0, keepdims=True)
        first = jnp.min(jnp.where(score == best, blk, n_blk), axis=0, keepdims=True)
        hit = blk == first
        sel = jnp.where(hit, 1.0, sel)
        score = jnp.where(hit, NEG_INF, score)
    sel_scr[...] = sel

    for j in range(n_grp + 1):
        list_scr[j] = 0
        valid_scr[j] = 0
    pos = jnp.int32(n_grp)
    for j in reversed(range(n_grp)):
        hit = jnp.logical_and(jnp.max(sel[j * bpg:(j + 1) * bpg, :]) > 0.0, j < diag_group)
        pos = pos - hit.astype(jnp.int32)
        slot = jnp.where(hit, pos, n_grp)
        list_scr[slot] = j
        valid_scr[slot] = 1
    n_pairs = (n_grp - pos + 1) // 2
    first = n_grp - 2 * n_pairs

    o_win = []
    win_valid = []
    for c in range(n_wc):
        dist = t_row - (start + c * ck + key_in_chunk)
        win_valid.append(jnp.logical_and(dist >= 0, dist < NSA_WINDOW))
    for r in range(n_rep):
        parts = [jnp.where(win_valid[c], s_win[r][c * ck:(c + 1) * ck], NEG_INF) for c in range(n_wc)]
        offs = [slopes[r] * (c * ck - _chunk_phase(start, c)).astype(F32) for c in range(n_wc)]
        p, _ = _softmax_cols_chunks(parts, offs)
        pv = jnp.dot(vw_band, p.astype(BF16), preferred_element_type=F32)
        den = pv[d:d + 1]
        o_win.append(pv[0:d] / jnp.where(den > 0, den, 1.0))

    m_scr[...] = jnp.full(m_scr.shape, NEG_INF, F32)
    acc_scr[...] = jnp.zeros(acc_scr.shape, F32)

    def reduce_group(gidx, live, s_scr, diagonal):
        k0 = pl.multiple_of(gidx * gsz, gsz)
        rows = sel_scr[pl.ds(pl.multiple_of(gidx * bpg, bpg), bpg), :] * live
        v_cat = vs_ref[0, 0, :, pl.ds(k0, gsz)]
        masks = []
        for u in range(bpg):
            picked = rows[u:u + 1, :] > 0.5
            if diagonal:
                picked = jnp.logical_and(picked, k0 + u * SEL_LEN + key_in_blk <= t_row)
            masks.append(picked)
        rel = (k0 - t0).astype(F32)
        for r in range(n_rep):
            parts = [jnp.where(masks[u], s_scr[r, pl.ds(u * SEL_LEN, SEL_LEN), :], NEG_INF) for u in range(bpg)]
            off = slopes[r] * rel
            top = parts[0]
            for part in parts[1:]:
                top = jnp.maximum(top, part)
            m_old = m_scr[r]
            m_new = jnp.maximum(m_old, jnp.max(top, axis=0, keepdims=True) + off)
            m_safe = jnp.where(m_new == NEG_INF, 0.0, m_new)
            alpha = jnp.exp2(m_old - m_safe)
            sub = m_safe - off
            p = jnp.concatenate([jnp.exp2(part - sub) for part in parts], axis=0)
            acc_scr[r] = alpha * acc_scr[r] + jnp.dot(v_cat, p.astype(BF16), preferred_element_type=F32)
            m_scr[r] = m_new

    score_group(list_scr[first], sa_scr)
    reduce_group(diag_group, 1.0, sb_scr, True)

    def body(pi, carry):
        a = first + 2 * pi
        score_group(list_scr[a + 1], sb_scr)
        reduce_group(list_scr[a], valid_scr[a].astype(F32), sa_scr, False)
        score_group(list_scr[a + 2], sa_scr)
        reduce_group(list_scr[a + 1], valid_scr[a + 1].astype(F32), sb_scr, False)
        return carry

    lax.fori_loop(0, n_pairs, body, 0)

    gates = _sigmoid(gl_ref[0, 0])
    outs = []
    for r in range(n_rep):
        acc = acc_scr[r]
        l_fin = acc[d:d + 1]
        o_slc = acc[0:d] / jnp.where(l_fin > 0, l_fin, 1.0)
        g0 = gates[N_BRANCH * r + 0:N_BRANCH * r + 1, :]
        g1 = gates[N_BRANCH * r + 1:N_BRANCH * r + 2, :]
        g2 = gates[N_BRANCH * r + 2:N_BRANCH * r + 3, :]
        outs.append(g0 * o_cmp[r] + g1 * o_slc + g2 * o_win[r])
    o_ref[0] = jnp.concatenate(outs, axis=0).astype(o_ref.dtype)


def _nsa_attention(slopes, q_t, slope_rows, kc, vc_t, ks, vs_t, kw, vw_t, gl_t, ov_w):
    b, g, r, d, t = q_t.shape
    n_blk = t // SEL_LEN
    n_grp = t // SEL_GROUP
    ncp = kc.shape[2]
    ka = kc.shape[3]
    va = vs_t.shape[2]
    idx4 = lambda bi, gi, i: (bi, gi, 0, 0)
    return pl.pallas_call(
        functools.partial(_nsa_kernel, n_rep=r, n_blk=n_blk),
        grid=(b, g, t // TQ),
        in_specs=[pl.BlockSpec(memory_space=pltpu.SMEM),
                  pl.BlockSpec((1, 1, r, d, TQ), lambda bi, gi, i: (bi, gi, 0, 0, i)),
                  pl.BlockSpec((1, r, ka - d, TQ), lambda bi, gi, i: (gi, 0, 0, 0)),
                  pl.BlockSpec((1, 1, ncp, ka), idx4),
                  pl.BlockSpec((1, 1, d, ncp), idx4),
                  pl.BlockSpec((1, 1, t, ka), idx4),
                  pl.BlockSpec((1, 1, va, t), idx4),
                  pl.BlockSpec((1, 1, t, ka), idx4),
                  pl.BlockSpec((1, 1, va, t), idx4),
                  pl.BlockSpec((1, 1, r * N_BRANCH, TQ), lambda bi, gi, i: (bi, gi, 0, i)),
                  pl.BlockSpec(ov_w.shape, lambda bi, gi, i: (0, 0))],
        out_specs=pl.BlockSpec((1, r * d, TQ), lambda bi, gi, i: (bi, gi, i)),
        out_shape=jax.ShapeDtypeStruct((b, g * r * d, t), BF16),
        scratch_shapes=[pltpu.VMEM((n_blk, TQ), F32),
                        pltpu.SMEM((n_grp + 2,), jnp.int32),
                        pltpu.SMEM((n_grp + 2,), jnp.int32),
                        pltpu.VMEM((r, SEL_GROUP, TQ), F32),
                        pltpu.VMEM((r, SEL_GROUP, TQ), F32),
                        pltpu.VMEM((r, 1, TQ), F32),
                        pltpu.VMEM((r, va, TQ), F32)],
        compiler_params=_params("parallel", "parallel", "arbitrary"),
        name="nsa_attention",
    )(slopes, q_t, slope_rows, kc, vc_t, ks, vs_t, kw, vw_t, gl_t, ov_w)


def _alibi_slopes(n_heads):
    return jnp.exp2(-8.0 * jnp.arange(1, n_heads + 1, dtype=F32) / n_heads) * LOG2E


def _truncation_split_bf16(x, n):
    parts = []
    for _ in range(n):
        top = lax.bitcast_convert_type(lax.bitcast_convert_type(x, jnp.uint32) & jnp.uint32(0xFFFF0000), F32)
        parts.append(top.astype(BF16))
        x = x - top
    return parts


def _slope_query_rows(n_heads, n_groups):
    parts = _truncation_split_bf16(_alibi_slopes(n_heads), 3)
    rows = jnp.zeros((n_heads, KEY_AUG), BF16)
    for j, p in enumerate(parts):
        rows = rows.at[:, j].set(p).at[:, 3 + j].set(p)
    rows = jnp.broadcast_to(rows[:, :, None], (n_heads, KEY_AUG, TQ))
    return rows.reshape(n_groups, n_heads // n_groups, KEY_AUG, TQ)


def _augment_keys(k_tm, lo, hi):
    b, g, n, _ = k_tm.shape
    extra = jnp.zeros((n, KEY_AUG), F32)
    extra = extra.at[:, 0:3].set(lo.astype(F32)[:, None]).at[:, 3:6].set(hi.astype(F32)[:, None]).astype(BF16)
    return jnp.concatenate([k_tm, jnp.broadcast_to(extra, (b, g, n, KEY_AUG))], axis=-1)


def _token_key_positions(t):
    pos = jnp.arange(t) % POS_PERIOD
    lo = pos % ATT_CHUNK
    return lo, pos - lo


def _heads_token_major(a, b, t, g):
    return a.reshape(b, t, g, HEAD_DIM).transpose(0, 2, 1, 3)


def _hybrid_layer(x3d, g_pre, g_post, w_in, w_out, sinks, mu, w0, w2, a0, a2, g2, k_k, k_a,
                  r_k, ln_g, ln_b):
    b, t, d = x3d.shape
    gq, rq = SWA_KV_HEADS, SWA_HEADS // SWA_KV_HEADS
    scale = HEAD_DIM ** -0.5 * LOG2E
    wq_t = (w_in[:, :SWA_Q_COLS] * scale).T.astype(BF16)
    no_const = jnp.zeros((SWA_Q_COLS, 1), F32)
    k0 = SWA_Q_COLS + SWA_KV_COLS
    q_fm, v_fm, k_tm, z_rwkv = _norm_proj(
        x3d, g_pre, [(wq_t, no_const), _value_weights(w_in[:, k0:SWA_COLS], gq)],
        w_in[:, SWA_Q_COLS:k0].astype(BF16), w_in[:, SWA_COLS:].astype(BF16))
    lo, hi = _token_key_positions(t)
    q_t = q_fm.reshape(b, gq, rq, HEAD_DIM, t)
    k = _augment_keys(_heads_token_major(k_tm.reshape(b * t, SWA_KV_COLS), b, t, gq), lo, hi)
    v_t = v_fm.reshape(b, gq, HEAD_DIM + VAL_AUG, t)
    o_a = _swa_attention(_alibi_slopes(SWA_HEADS), sinks.astype(F32), q_t, _slope_query_rows(SWA_HEADS, gq), k, v_t)
    o_b = _rwkv_time_mix(z_rwkv, mu, w0, w2, a0, a2, g2, k_k, k_a, r_k.reshape(-1), ln_g, ln_b)
    w_out = w_out.astype(BF16)
    return _out_proj([o_a, o_b], [True, False], [w_out[:SWA_Q_COLS], w_out[SWA_Q_COLS:]], x3d, g_post)


def _overlap_weights(n_blk, n_cmp_pad, n_cmp):
    ratio, span = SEL_LEN // CMP_STRIDE, CMP_LEN // CMP_STRIDE
    w = np.zeros((n_blk, n_cmp_pad), np.float32)
    for j in range(n_blk):
        for a in range(ratio):
            for s in range(span):
                c = ratio * j + a - s
                if 0 <= c < n_cmp:
                    w[j, c] += 1.0
    return jnp.asarray(w, BF16)


def _nsa_layer(x3d, g_pre, g_post, w_in, w_out, pos_k, w1_k, w2_k, pos_v, w1_v, w2_v):
    b, t, d = x3d.shape
    g, r = NSA_KV_HEADS, NSA_HEADS // NSA_KV_HEADS
    off = NSA_Q_COLS
    kv = NSA_KV_COLS
    scale = HEAD_DIM ** -0.5 * LOG2E
    wq_t = (w_in[:, :off] * scale).T.astype(BF16)
    w_cols = lambda j: w_in[:, off + j * kv:off + (j + 1) * kv]
    w_f = jnp.concatenate([w_cols(0), w_cols(1), w_in[:, off + 6 * kv:]], axis=1)
    w_f = jnp.pad(w_f, ((0, 0), (0, (-w_f.shape[1]) % LANES))).astype(BF16)
    w_b = jnp.concatenate([w_cols(2), w_cols(4)], axis=1).astype(BF16)
    q_fm, vs_fm, vw_fm, z_b, z_f = _norm_proj(
        x3d, g_pre, [(wq_t, jnp.zeros((off, 1), F32)), _value_weights(w_cols(3), g), _value_weights(w_cols(5), g)],
        w_b, w_f)
    z_b = z_b.reshape(b * t, 2 * kv)
    z_f = z_f.reshape(b * t, -1)
    cols = lambda j: z_f[:, j * kv:(j + 1) * kv]
    q_t = q_fm.reshape(b, g, r, HEAD_DIM, t)
    n_half = t // CMP_STRIDE
    n_cmp = (t - CMP_LEN) // CMP_STRIDE + 1
    half_rows = lambda a: a.reshape(b, n_half, CMP_STRIDE, g, HEAD_DIM).transpose(0, 3, 1, 2, 4).reshape(
        b, g, n_half, CMP_STRIDE * HEAD_DIM)
    flat_pos = lambda p: p.reshape(CMP_LEN // CMP_STRIDE, CMP_STRIDE * HEAD_DIM)
    flat_w1 = lambda w: w.reshape(CMP_LEN * HEAD_DIM, CMP_HIDDEN).astype(BF16)
    kc, vc_t = _compress(half_rows(cols(0)), half_rows(cols(1)), flat_pos(pos_k), flat_pos(pos_v),
                         flat_w1(w1_k), flat_w1(w1_v), w2_k.astype(BF16), w2_v.astype(BF16), n_cmp)
    lo, hi = _token_key_positions(t)
    cmp_pos = jnp.arange(n_half) * CMP_STRIDE
    kc = _augment_keys(kc, cmp_pos, jnp.zeros_like(cmp_pos))
    ks = _augment_keys(_heads_token_major(z_b[:, :kv], b, t, g), lo, hi)
    vs_t = vs_fm.reshape(b, g, HEAD_DIM + VAL_AUG, t)
    kw = _augment_keys(_heads_token_major(z_b[:, kv:], b, t, g), lo, hi)
    vw_t = vw_fm.reshape(b, g, HEAD_DIM + VAL_AUG, t)
    gl_t = z_f[:, 2 * kv:2 * kv + NSA_HEADS * N_BRANCH].reshape(b, t, g, r * N_BRANCH).transpose(0, 2, 3, 1)
    ov_w = _overlap_weights(t // SEL_LEN, n_half, n_cmp)
    o = _nsa_attention(_alibi_slopes(NSA_HEADS), q_t, _slope_query_rows(NSA_HEADS, g), kc, vc_t, ks, vs_t,
                       kw, vw_t, gl_t, ov_w)
    return _out_proj([o], [True], [w_out.astype(BF16)], x3d, g_post)


def kernel(x, mix_pre_g, mix_post_g, ffn_pre_g, ffn_post_g, hy_w_in, hy_w_out, swa_sinks, rwkv_mu, rwkv_w0, rwkv_w2, rwkv_a0, rwkv_a2, rwkv_g2, rwkv_k_k, rwkv_k_a, rwkv_r_k, rwkv_ln_g, rwkv_ln_b, nsa_w_in, nsa_w_out, nsa_cmp_pos_k, nsa_cmp_w1_k, nsa_cmp_w2_k, nsa_cmp_pos_v, nsa_cmp_w1_v, nsa_cmp_w2_v, ffn_w_up, ffn_conv_w, ffn_conv_b, ffn_w_down):
    b, t, d = x.shape
    depth = mix_pre_g.shape[0]
    for layer in range(depth):
        i = layer // 2
        if layer % 2 == 0:
            x = _hybrid_layer(x, mix_pre_g[layer], mix_post_g[layer], hy_w_in[i], hy_w_out[i],
                              swa_sinks[i], rwkv_mu[i], rwkv_w0[i], rwkv_w2[i], rwkv_a0[i], rwkv_a2[i],
                              rwkv_g2[i], rwkv_k_k[i], rwkv_k_a[i], rwkv_r_k[i], rwkv_ln_g[i], rwkv_ln_b[i])
        else:
            x = _nsa_layer(x, mix_pre_g[layer], mix_post_g[layer], nsa_w_in[i], nsa_w_out[i],
                           nsa_cmp_pos_k[i], nsa_cmp_w1_k[i], nsa_cmp_w2_k[i],
                           nsa_cmp_pos_v[i], nsa_cmp_w1_v[i], nsa_cmp_w2_v[i])
        x = _conv_ffn(x.reshape(b * t, d), t, ffn_pre_g[layer], ffn_w_up[layer].astype(BF16), ffn_conv_w[layer],
                      ffn_conv_b[layer], ffn_w_down[layer].astype(BF16), ffn_post_g[layer]).reshape(b, t, d)
    return x
```
